```python
import math
import jax
import jax.numpy as jnp
from jax import lax
import numpy as np

D_MODEL = 1024
BATCH = 8
SEQ = 4096
DEPTH = 4

CTX_LEN = 256
GRID_W = 64
ROPE_BASE = 10000.0
Q_BLOCK = 128
RMS_EPS = 1e-6
LN_EPS = 1e-5

A_WIDTH = 512
A_KERNEL = 31
B_HEADS = 4
B_HEAD_DIM = 64
B_V_DIM = 2 * B_HEAD_DIM
B_QK_COLS = B_HEADS * 2 * B_HEAD_DIM
B_WIDTH = B_HEADS * B_V_DIM
AB_IN = 2 * A_WIDTH + 2 * B_QK_COLS + B_WIDTH
AB_OUT = A_WIDTH + B_WIDTH
C_HEADS = 16
C_NOPE = 64
C_ROPE = 32
C_V = 64
C_Q_RANK = 768
C_KV_RANK = 256
C_IN = C_Q_RANK + C_KV_RANK + C_ROPE
FFN_HIDDEN = 2816
FFN_KERNEL = 3

N_EVEN = (DEPTH + 1) // 2
N_ODD = DEPTH // 2

kernel_name = 'hybrid_conv_diffattn_mla_dit_trunk'


def rms_norm(x, g):
    xf = x.astype(jnp.float32)
    y = xf * lax.rsqrt(jnp.mean(xf * xf, axis=-1, keepdims=True) + RMS_EPS)
    return (y * g.astype(jnp.float32)).astype(x.dtype)


def layer_norm(x, g, b):
    xf = x.astype(jnp.float32)
    mu = jnp.mean(xf, axis=-1, keepdims=True)
    var = jnp.mean(jnp.square(xf - mu), axis=-1, keepdims=True)
    y = (xf - mu) * lax.rsqrt(var + LN_EPS)
    return (y * g.astype(jnp.float32) + b.astype(jnp.float32)).astype(x.dtype)


def modulate(h, shift, scale):
    return h * (1 + scale) + shift


def rope_1d(x, pos):
    half = x.shape[-1] // 2
    inv_freq = ROPE_BASE ** (-jnp.arange(half, dtype=jnp.float32) / half)
    ang = pos.astype(jnp.float32)[:, None] * inv_freq[None, :]
    cos = jnp.cos(ang).astype(x.dtype)
    sin = jnp.sin(ang).astype(x.dtype)
    x1, x2 = x[..., :half], x[..., half:]
    return jnp.concatenate([x1 * cos - x2 * sin, x1 * sin + x2 * cos], axis=-1)


def rope_2d(x):
    n_tokens = x.shape[-2]
    rows_count = n_tokens // GRID_W
    rows = jnp.repeat(jnp.arange(rows_count, dtype=jnp.int32), GRID_W)
    cols = jnp.tile(jnp.arange(GRID_W, dtype=jnp.int32), rows_count)
    r = x.shape[-1] // 2
    return jnp.concatenate([rope_1d(x[..., :r], rows), rope_1d(x[..., r:], cols)], axis=-1)


def depthwise_conv(x, w, b):
    y = lax.conv_general_dilated(x, w[:, None, :].astype(x.dtype), window_strides=(1,), padding='SAME',
                                 dimension_numbers=('NWC', 'WIO', 'NWC'), feature_group_count=x.shape[-1])
    return y + b.astype(x.dtype)


def sweep_query_blocks(fn, *qs):
    bsz, heads, s = qs[0].shape[:3]
    nblk = s // Q_BLOCK
    blocks = tuple(q.reshape(bsz, heads, nblk, Q_BLOCK, q.shape[-1]).transpose(2, 0, 1, 3, 4) for q in qs)
    out = lax.map(lambda blk: fn(*blk), blocks)
    return out.transpose(1, 2, 0, 3, 4).reshape(bsz, heads, s, out.shape[-1])


def diff_attn_block(q1, q2, k1, k2, v, lam):
    scale = B_HEAD_DIM ** -0.5
    s1 = jnp.einsum('bhqd,bhkd->bhqk', q1, k1).astype(jnp.float32) * scale
    s2 = jnp.einsum('bhqd,bhkd->bhqk', q2, k2).astype(jnp.float32) * scale
    p = jax.nn.softmax(s1, axis=-1) - lam * jax.nn.softmax(s2, axis=-1)
    return jnp.einsum('bhqk,bhkd->bhqd', p.astype(v.dtype), v)


def mla_block(q_nope, q_pe, k_nope, k_pe, v):
    scale = (C_NOPE + C_ROPE) ** -0.5
    s = (jnp.einsum('bhqd,bhkd->bhqk', q_nope, k_nope)
         + jnp.einsum('bhqr,bkr->bhqk', q_pe, k_pe)).astype(jnp.float32) * scale
    p = jax.nn.softmax(s, axis=-1)
    return jnp.einsum('bhqk,bhkd->bhqd', p.astype(v.dtype), v)


def conv_ffn(h, w_up, conv_w, conv_b, w_down):
    u = h @ w_up
    val, gate = u[..., :FFN_HIDDEN], u[..., FFN_HIDDEN:]
    gate = depthwise_conv(gate, conv_w, conv_b)
    return (jax.nn.silu(gate) * val) @ w_down


def mixer_conv_diff(hx, hc, w_in, conv_w, conv_b, ln_g, ln_b, lam_vecs, subln_g, w_out, lambda_init, with_ctx):
    lv = lam_vecs.astype(jnp.float32)
    lam = jnp.exp(jnp.sum(lv[0] * lv[1])) - jnp.exp(jnp.sum(lv[2] * lv[3])) + lambda_init

    def split(h):
        bsz, t = h.shape[:2]
        p = h @ w_in
        a = p[..., :2 * A_WIDTH]
        o = 2 * A_WIDTH
        q = p[..., o:o + B_QK_COLS].reshape(bsz, t, B_HEADS, 2, B_HEAD_DIM).transpose(3, 0, 2, 1, 4)
        k = p[..., o + B_QK_COLS:o + 2 * B_QK_COLS].reshape(bsz, t, B_HEADS, 2, B_HEAD_DIM).transpose(3, 0, 2, 1, 4)
        v = p[..., o + 2 * B_QK_COLS:].reshape(bsz, t, B_HEADS, B_V_DIM).transpose(0, 2, 1, 3)
        return a, q, k, v

    def conv_module(a):
        u = a[..., :A_WIDTH] * jax.nn.sigmoid(a[..., A_WIDTH:])
        u = depthwise_conv(u, conv_w, conv_b)
        return jax.nn.silu(layer_norm(u, ln_g, ln_b))

    def merge(ya, ob):
        bsz, _, t, _ = ob.shape
        yb = (rms_norm(ob, subln_g) * (1 - lambda_init)).transpose(0, 2, 1, 3).reshape(bsz, t, B_WIDTH)
        return jnp.concatenate([ya, yb], axis=-1) @ w_out

    ax, qx, kx, vx = split(hx)
    ac, qc, kc, vc = split(hc)
    qx = rope_2d(qx)
    kx = rope_2d(kx)
    k1 = jnp.concatenate([kc[0], kx[0]], axis=2)
    k2 = jnp.concatenate([kc[1], kx[1]], axis=2)
    v_all = jnp.concatenate([vc, vx], axis=2)
    ob_x = sweep_query_blocks(lambda q1, q2: diff_attn_block(q1, q2, k1, k2, v_all, lam), qx[0], qx[1])
    yx = merge(conv_module(ax), ob_x)
    if not with_ctx:
        return yx, None
    ob_c = diff_attn_block(qc[0], qc[1], kc[0], kc[1], vc, lam)
    yc = merge(conv_module(ac), ob_c)
    return yx, yc


def mixer_mla(hx, hc, w_in, q_norm_g, kv_norm_g, w_uq, w_ukv, w_out, with_ctx):
    def project(h, rotate):
        bsz, t = h.shape[:2]
        p = h @ w_in
        cq = rms_norm(p[..., :C_Q_RANK], q_norm_g)
        ckv = rms_norm(p[..., C_Q_RANK:C_Q_RANK + C_KV_RANK], kv_norm_g)
        k_pe = p[..., C_Q_RANK + C_KV_RANK:]
        q = (cq @ w_uq).reshape(bsz, t, C_HEADS, C_NOPE + C_ROPE).transpose(0, 2, 1, 3)
        kv = (ckv @ w_ukv).reshape(bsz, t, C_HEADS, C_NOPE + C_V).transpose(0, 2, 1, 3)
        q_nope, q_pe = q[..., :C_NOPE], q[..., C_NOPE:]
        k_nope, v = kv[..., :C_NOPE], kv[..., C_NOPE:]
        if rotate:
            q_pe = rope_2d(q_pe)
            k_pe = rope_2d(k_pe)
        return q_nope, q_pe, k_nope, k_pe, v

    def out_proj(o):
        bsz, _, t, _ = o.shape
        return o.transpose(0, 2, 1, 3).reshape(bsz, t, C_HEADS * C_V) @ w_out

    qnx, qpx, knx, kpx, vx = project(hx, True)
    qnc, qpc, knc, kpc, vc = project(hc, False)
    kn_all = jnp.concatenate([knc, knx], axis=2)
    kp_all = jnp.concatenate([kpc, kpx], axis=1)
    v_all = jnp.concatenate([vc, vx], axis=2)
    ox = sweep_query_blocks(lambda qn, qp: mla_block(qn, qp, kn_all, kp_all, v_all), qnx, qpx)
    yx = out_proj(ox)
    if not with_ctx:
        return yx, None
    oc = mla_block(qnc, qpc, knc, kpc, vc)
    return yx, out_proj(oc)


def setup_inputs(seed: int = 0) -> dict:
    key = jax.random.key(seed)
    ks = iter(jax.random.split(key, 40))

    def nrm(shape, scale):
        return jax.random.normal(next(ks), shape, jnp.float32) * scale

    def gain(shape):
        return 1.0 + nrm(shape, 0.05)

    D = D_MODEL
    F = FFN_HIDDEN
    return {
        'x': nrm((BATCH, SEQ, D), 1.0),
        'c': nrm((BATCH, D), 1.0),
        'ctx': nrm((BATCH, CTX_LEN, D), 1.0),
        'c_ctx': nrm((D,), 1.0),
        'ada_w': nrm((DEPTH, D, 6 * D), 0.5 * D ** -0.5),
        'ada_b': nrm((DEPTH, 6 * D), 0.02),
        'norm_mix_pre': gain((DEPTH, D)),
        'norm_mix_post': gain((DEPTH, D)),
        'norm_ffn_pre': gain((DEPTH, D)),
        'norm_ffn_post': gain((DEPTH, D)),
        'ffn_w_up': nrm((DEPTH, D, 2 * F), D ** -0.5),
        'ffn_conv_w': nrm((DEPTH, FFN_KERNEL, F), FFN_KERNEL ** -0.5),
        'ffn_conv_b': nrm((DEPTH, F), 0.02),
        'ffn_w_down': nrm((DEPTH, F, D), F ** -0.5),
        'ab_w_in': nrm((N_EVEN, D, AB_IN), D ** -0.5),
        'a_conv_w': nrm((N_EVEN, A_KERNEL, A_WIDTH), A_KERNEL ** -0.5),
        'a_conv_b': nrm((N_EVEN, A_WIDTH), 0.02),
        'a_ln_g': gain((N_EVEN, A_WIDTH)),
        'a_ln_b': nrm((N_EVEN, A_WIDTH), 0.02),
        'b_lambda': nrm((N_EVEN, 4, B_HEAD_DIM), 0.1),
        'b_subln': gain((N_EVEN, B_V_DIM)),
        'ab_w_out': nrm((N_EVEN, AB_OUT, D), AB_OUT ** -0.5),
        'c_w_in': nrm((N_ODD, D, C_IN), D ** -0.5),
        'c_q_norm': gain((N_ODD, C_Q_RANK)),
        'c_kv_norm': gain((N_ODD, C_KV_RANK)),
        'c_w_uq': nrm((N_ODD, C_Q_RANK, C_HEADS * (C_NOPE + C_ROPE)), C_Q_RANK ** -0.5),
        'c_w_ukv': nrm((N_ODD, C_KV_RANK, C_HEADS * (C_NOPE + C_V)), C_KV_RANK ** -0.5),
        'c_w_out': nrm((N_ODD, C_HEADS * C_V, D), (C_HEADS * C_V) ** -0.5),
    }


def reference(x, c, ctx, c_ctx, ada_w, ada_b, norm_mix_pre, norm_mix_post, norm_ffn_pre, norm_ffn_post,
              ffn_w_up, ffn_conv_w, ffn_conv_b, ffn_w_down, ab_w_in, a_conv_w, a_conv_b, a_ln_g, a_ln_b,
              b_lambda, b_subln, ab_w_out, c_w_in, c_q_norm, c_kv_norm, c_w_uq, c_w_ukv, c_w_out):
    silu_c = jax.nn.silu(c)
    silu_cc = jax.nn.silu(c_ctx)
    for i in range(DEPTH):
        with_ctx = i < DEPTH - 1
        mx = jnp.split((silu_c @ ada_w[i] + ada_b[i])[:, None, :], 6, axis=-1)
        mc = jnp.split((silu_cc @ ada_w[i] + ada_b[i])[None, None, :], 6, axis=-1)
        hx = modulate(rms_norm(x, norm_mix_pre[i]), mx[0], mx[1])
        hc = modulate(rms_norm(ctx, norm_mix_pre[i]), mc[0], mc[1])
        j = i // 2
        if i % 2 == 0:
            lambda_init = 0.8 - 0.6 * math.exp(-0.3 * i)
            yx, yc = mixer_conv_diff(hx, hc, ab_w_in[j], a_conv_w[j], a_conv_b[j], a_ln_g[j], a_ln_b[j],
                                     b_lambda[j], b_subln[j], ab_w_out[j], lambda_init, with_ctx)
        else:
            yx, yc = mixer_mla(hx, hc, c_w_in[j], c_q_norm[j], c_kv_norm[j], c_w_uq[j], c_w_ukv[j],
                               c_w_out[j], with_ctx)
        x = x + mx[2] * rms_norm(yx, norm_mix_post[i])
        hx = modulate(rms_norm(x, norm_ffn_pre[i]), mx[3], mx[4])
        x = x + mx[5] * rms_norm(conv_ffn(hx, ffn_w_up[i], ffn_conv_w[i], ffn_conv_b[i], ffn_w_down[i]),
                                 norm_ffn_post[i])
        if with_ctx:
            ctx = ctx + mc[2] * rms_norm(yc, norm_mix_post[i])
            hc = modulate(rms_norm(ctx, norm_ffn_pre[i]), mc[3], mc[4])
            ctx = ctx + mc[5] * rms_norm(conv_ffn(hc, ffn_w_up[i], ffn_conv_w[i], ffn_conv_b[i], ffn_w_down[i]),
                                         norm_ffn_post[i])
    return x
```

```python
import functools
import math

import jax
import jax.numpy as jnp
from jax import lax
from jax.experimental import pallas as pl
from jax.experimental.pallas import tpu as pltpu

D_MODEL = 1024
BATCH = 8
SEQ = 4096
DEPTH = 4
CTX_LEN = 256
GRID_W = 64
ROPE_BASE = 10000.0
RMS_EPS = 1e-6
LN_EPS = 1e-5

A_WIDTH = 512
A_KERNEL = 31
B_HEADS = 4
B_HEAD_DIM = 64
B_V_DIM = 2 * B_HEAD_DIM
B_QK_COLS = B_HEADS * 2 * B_HEAD_DIM
B_WIDTH = B_HEADS * B_V_DIM
AB_IN = 2 * A_WIDTH + 2 * B_QK_COLS + B_WIDTH

C_HEADS = 16
C_NOPE = 64
C_ROPE = 32
C_V = 64
C_Q_RANK = 768
C_KV_RANK = 256

FFN_HIDDEN = 2816
FFN_KERNEL = 3

LANES = 128
SUBLANES = 8
BF16_SUBLANES = 16
MXU_COLS = 256
VMEM_LIMIT_BYTES = 56 * 1024 * 1024

TOK = CTX_LEN + SEQ
NT = BATCH * TOK
TM = 256
TPB = TOK // TM
NTILES = NT // TM
MOD_ROWS = 16
CTX_ROW = BATCH
HEAD_PAD = LANES
C_HEADS_PER_STEP = 4
KV_CHUNK = 1024
FFN_CHUNK = MXU_COLS
A_HALO = 16
F_HALO = SUBLANES

F32 = jnp.float32
BF16 = jnp.bfloat16


def _dot(a, b):
    return jnp.dot(a, b, preferred_element_type=F32)


def _dot_nt(a, b):
    return lax.dot_general(a, b, (((1,), (1,)), ((), ())), preferred_element_type=F32)


def _rms(x, eps=RMS_EPS):
    return x * lax.rsqrt(jnp.mean(x * x, axis=-1, keepdims=True) + eps)


def _silu(x):
    return x * jax.nn.sigmoid(x)


def _params(n_axes):
    return pltpu.CompilerParams(dimension_semantics=("parallel",) * n_axes, vmem_limit_bytes=VMEM_LIMIT_BYTES)


def _full(shape):
    return pl.BlockSpec(shape, lambda *_: (0,) * len(shape))


def _tile_in_batch(t):
    return t % TPB


def _mod_spec(layer, slot):
    def index(t):
        row = jnp.where(_tile_in_batch(t) == 0, CTX_ROW, t // TPB)
        return ((layer * MOD_ROWS + row) * 6 + slot, 0, 0)
    return pl.BlockSpec((None, 1, D_MODEL), index)


def _tok_spec(width):
    return pl.BlockSpec((TM, width), lambda t: (t, 0))


def _rope_spec():
    return pl.BlockSpec((TM, LANES), lambda t: (_tile_in_batch(t), 0))


def _rope(y, c, s_up, s_dn, shift):
    return y * c + pltpu.roll(y, LANES - shift, 1) * s_up + pltpu.roll(y, shift, 1) * s_dn


def _rope_tables(group, lane_offset, period):
    half = group // 4
    lane = jnp.arange(LANES)
    rel = (lane % period) - lane_offset
    active = (rel >= 0) & (rel < group)
    rel = jnp.where(active, rel, 0)
    by_col = rel >= group // 2
    within = rel % (group // 2)
    freq_idx = within % half
    inv_freq = ROPE_BASE ** (-freq_idx.astype(F32) / half)
    t = jnp.arange(SEQ, dtype=jnp.int32)
    pos = jnp.where(by_col[None, :], (t % GRID_W)[:, None], (t // GRID_W)[:, None]).astype(F32)
    ang = pos * inv_freq[None, :]
    cos, sin = jnp.cos(ang), jnp.sin(ang)
    first = within < half
    c = jnp.where(active[None, :], cos, 1.0)
    s_up = jnp.where((active & first)[None, :], -sin, 0.0)
    s_dn = jnp.where((active & ~first)[None, :], sin, 0.0)
    ident = [jnp.ones((CTX_LEN, LANES), F32), jnp.zeros((CTX_LEN, LANES), F32), jnp.zeros((CTX_LEN, LANES), F32)]
    return tuple(jnp.concatenate([i, x.astype(F32)], axis=0) for i, x in zip(ident, (c, s_up, s_dn)))


def _mods_kernel(c_ref, w_ref, b_ref, o_ref):
    s = _silu(c_ref[...])
    o_ref[...] = jnp.dot(s, w_ref[...], preferred_element_type=F32, precision=lax.Precision.HIGHEST) + b_ref[...]


def _modulation(c, c_ctx, ada_w, ada_b):
    rows = jnp.concatenate([c, c_ctx[None, :], jnp.zeros((MOD_ROWS - BATCH - 1, D_MODEL), F32)], axis=0)
    n_blk = 1536
    out = pl.pallas_call(
        _mods_kernel,
        grid=(DEPTH, 6 * D_MODEL // n_blk),
        in_specs=[
            pl.BlockSpec((MOD_ROWS, D_MODEL), lambda i, n: (0, 0)),
            pl.BlockSpec((None, D_MODEL, n_blk), lambda i, n: (i, 0, n)),
            pl.BlockSpec((None, 1, n_blk), lambda i, n: (i, 0, n)),
        ],
        out_specs=pl.BlockSpec((None, MOD_ROWS, n_blk), lambda i, n: (i, 0, n)),
        out_shape=jax.ShapeDtypeStruct((DEPTH, MOD_ROWS, 6 * D_MODEL), F32),
        compiler_params=_params(2),
        name="modulation",
    )(rows, ada_w, ada_b.reshape(DEPTH, 1, 6 * D_MODEL))
    return out.reshape(DEPTH * MOD_ROWS * 6, 1, D_MODEL)


def _pre_norm(x, g_ref, sc_ref, sh_ref):
    return _rms(x) * (g_ref[...] * (1.0 + sc_ref[...])) + sh_ref[...]


def _p1_even_kernel(x_ref, sh_ref, sc_ref, g_ref, w_ref, rc_ref, ru_ref, rd_ref, a_ref, q_ref, k_ref, v_ref):
    hb = _pre_norm(x_ref[...], g_ref, sc_ref, sh_ref).astype(BF16)
    a_ref[...] = _dot(hb, w_ref[:, 0:2 * A_WIDTH]).astype(BF16)
    c, s_up, s_dn = rc_ref[...], ru_ref[...], rd_ref[...]
    o = 2 * A_WIDTH
    for col, out_ref, scale in ((o, q_ref, B_HEAD_DIM ** -0.5), (o + B_QK_COLS, k_ref, 1.0)):
        y = _dot(hb, w_ref[:, col:col + B_QK_COLS])
        for j in range(B_QK_COLS // LANES):
            yj = _rope(y[:, j * LANES:(j + 1) * LANES], c, s_up, s_dn, B_HEAD_DIM // 4)
            out_ref[:, j * LANES:(j + 1) * LANES] = (yj * scale).astype(BF16)
    v_ref[...] = _dot(hb, w_ref[:, o + 2 * B_QK_COLS:AB_IN]).astype(BF16)


def _p1_even(layer, xs, mods, g, w_in, rope):
    return pl.pallas_call(
        _p1_even_kernel,
        grid=(NTILES,),
        in_specs=[_tok_spec(D_MODEL), _mod_spec(layer, 0), _mod_spec(layer, 1), _full((1, D_MODEL)),
                  _full((D_MODEL, AB_IN)), _rope_spec(), _rope_spec(), _rope_spec()],
        out_specs=[_tok_spec(2 * A_WIDTH), _tok_spec(B_QK_COLS), _tok_spec(B_QK_COLS), _tok_spec(B_WIDTH)],
        out_shape=[jax.ShapeDtypeStruct((NT, 2 * A_WIDTH), BF16), jax.ShapeDtypeStruct((NT, B_QK_COLS), BF16),
                   jax.ShapeDtypeStruct((NT, B_QK_COLS), BF16), jax.ShapeDtypeStruct((NT, B_WIDTH), BF16)],
        compiler_params=_params(1),
        name="p1_even",
    )(xs, mods, mods, g, w_in, *rope)


def _softmax_step(s, v, m_ref, l_ref, acc_ref):
    reps = s.shape[1] // LANES
    m_prev = m_ref[...]
    m_new = jnp.maximum(m_prev, jnp.max(s, axis=1, keepdims=True))
    alpha = jnp.exp(m_prev - m_new)
    p = jnp.exp(s - pltpu.repeat(m_new, reps, 1))
    l_ref[...] = alpha * l_ref[...] + jnp.sum(p, axis=1, keepdims=True)
    acc_ref[...] = alpha * acc_ref[...] + _dot(p.astype(BF16), v)
    m_ref[...] = m_new


def _for_each_kv_chunk(process):
    process(0, CTX_LEN)

    @pl.when(pl.program_id(2) > 0)
    def _():
        def body(i, carry):
            process(pl.multiple_of(CTX_LEN + i * KV_CHUNK, CTX_LEN), KV_CHUNK)
            return carry
        lax.fori_loop(0, SEQ // KV_CHUNK, body, 0)


def _diff_attn_kernel(lam_ref, sub_ref, q_ref, k_ref, v_ref, o_ref, m_ref, l_ref, acc_ref, *, lambda_init):
    q = q_ref[...]
    lane = lax.broadcasted_iota(jnp.int32, q.shape, 1)
    zero = jnp.zeros_like(q)
    qs = (jnp.where(lane < B_HEAD_DIM, q, zero), jnp.where(lane >= B_HEAD_DIM, q, zero))
    m_ref[...] = jnp.full(m_ref.shape, -jnp.inf, F32)
    l_ref[...] = jnp.zeros(l_ref.shape, F32)
    acc_ref[...] = jnp.zeros(acc_ref.shape, F32)

    def process(start, size):
        kc = k_ref[pl.ds(start, size), :]
        vc = v_ref[pl.ds(start, size), :]
        for i in range(2):
            _softmax_step(_dot_nt(qs[i], kc), vc, m_ref.at[i], l_ref.at[i], acc_ref.at[i])

    _for_each_kv_chunk(process)

    lv = lam_ref[...]
    lam = (jnp.exp(jnp.sum(lv[0:1] * lv[1:2], axis=-1, keepdims=True))
           - jnp.exp(jnp.sum(lv[2:3] * lv[3:4], axis=-1, keepdims=True)) + lambda_init)
    o = acc_ref[0] / l_ref[0] - lam * (acc_ref[1] / l_ref[1])
    o_ref[...] = (_rms(o) * (sub_ref[...] * (1.0 - lambda_init))).astype(BF16)


def _diff_attn(q, k, v, lam_vecs, subln_g, lambda_init):
    tok = lambda width: jax.ShapeDtypeStruct((BATCH, TOK, width), BF16)
    kv_spec = pl.BlockSpec((None, TOK, LANES), lambda b, h, i: (b, 0, h))
    return pl.pallas_call(
        functools.partial(_diff_attn_kernel, lambda_init=lambda_init),
        grid=(BATCH, B_HEADS, TPB),
        in_specs=[pl.BlockSpec((4, B_HEAD_DIM), lambda b, h, i: (0, 0)),
                  pl.BlockSpec((1, B_V_DIM), lambda b, h, i: (0, 0)),
                  pl.BlockSpec((None, TM, LANES), lambda b, h, i: (b, i, h)), kv_spec, kv_spec],
        out_specs=pl.BlockSpec((None, TM, B_V_DIM), lambda b, h, i: (b, i, h)),
        out_shape=tok(B_WIDTH),
        scratch_shapes=[pltpu.VMEM((2, TM, LANES), F32), pltpu.VMEM((2, TM, LANES), F32),
                        pltpu.VMEM((2, TM, B_V_DIM), F32)],
        compiler_params=_params(3),
        name="diff_attn",
    )(lam_vecs, subln_g, q, k, v)


def _post_even_kernel(a_ref, ap_ref, an_ref, ob_ref, x_ref, gate_ref, w_ref, cw_ref, cb_ref, lg_ref, lb_ref,
                      gp_ref, o_ref, u_ref):
    r = _tile_in_batch(pl.program_id(0))
    prev_ok = (r >= 2).astype(F32)
    next_ok = ((r >= 1) & (r <= TPB - 2)).astype(F32)

    def glu(a):
        a = a.astype(F32)
        return a[:, :A_WIDTH] * jax.nn.sigmoid(a[:, A_WIDTH:])

    u_ref[0:A_HALO, :] = glu(ap_ref[...]) * prev_ok
    u_ref[A_HALO:A_HALO + TM, :] = glu(a_ref[...])
    u_ref[A_HALO + TM:, :] = glu(an_ref[...]) * next_ok

    rows = 32
    base = A_HALO - A_KERNEL // 2
    outs = []
    for r0 in range(0, TM, rows):
        acc = jnp.broadcast_to(cb_ref[...], (rows, A_WIDTH))
        for tap in range(A_KERNEL):
            acc = acc + cw_ref[tap:tap + 1, :] * u_ref[r0 + base + tap:r0 + base + tap + rows, :]
        mu = jnp.mean(acc, axis=-1, keepdims=True)
        cen = acc - mu
        var = jnp.mean(cen * cen, axis=-1, keepdims=True)
        y = cen * lax.rsqrt(var + LN_EPS) * lg_ref[...] + lb_ref[...]
        outs.append(_silu(y).astype(BF16))
    ya = jnp.concatenate(outs, axis=0)
    y = _dot(ya, w_ref[0:A_WIDTH, :]) + _dot(ob_ref[...], w_ref[A_WIDTH:, :])
    o_ref[...] = x_ref[...] + gate_ref[...] * (_rms(y) * gp_ref[...])


def _post_even(layer, a, ob, xs, mods, w_out, conv_w, conv_b, ln_g, ln_b, g_post):
    blocks = TM // A_HALO
    return pl.pallas_call(
        _post_even_kernel,
        grid=(NTILES,),
        in_specs=[_tok_spec(2 * A_WIDTH),
                  pl.BlockSpec((A_HALO, 2 * A_WIDTH), lambda t: (jnp.maximum(t * blocks - 1, 0), 0)),
                  pl.BlockSpec((A_HALO, 2 * A_WIDTH), lambda t: (jnp.minimum((t + 1) * blocks, NT // A_HALO - 1), 0)),
                  _tok_spec(B_WIDTH), _tok_spec(D_MODEL), _mod_spec(layer, 2),
                  _full((A_WIDTH + B_WIDTH, D_MODEL)), _full((A_KERNEL + 1, A_WIDTH)), _full((1, A_WIDTH)),
                  _full((1, A_WIDTH)), _full((1, A_WIDTH)), _full((1, D_MODEL))],
        out_specs=_tok_spec(D_MODEL),
        out_shape=jax.ShapeDtypeStruct((NT, D_MODEL), F32),
        scratch_shapes=[pltpu.VMEM((TM + 2 * A_HALO, A_WIDTH), F32)],
        compiler_params=_params(1),
        name="post_even",
    )(a, a, a, ob, xs, mods, w_out, conv_w, conv_b, ln_g, ln_b, g_post)


def _p1_odd_kernel(x_ref, sh_ref, sc_ref, g_ref, w_ref, gq_ref, gkv_ref, wq_ref, wk_ref, wv_ref,
                   rc_ref, ru_ref, rd_ref, q_ref, k_ref, v_ref):
    hb = _pre_norm(x_ref[...], g_ref, sc_ref, sh_ref).astype(BF16)
    p = _dot(hb, w_ref[...])
    c, s_up, s_dn = rc_ref[...], ru_ref[...], rd_ref[...]
    shift = C_ROPE // 4
    scale = (C_NOPE + C_ROPE) ** -0.5

    cq = (_rms(p[:, :C_Q_RANK]) * gq_ref[...]).astype(BF16)
    q = _dot(cq, wq_ref[...])
    for j in range(C_HEADS):
        sl = slice(j * HEAD_PAD, (j + 1) * HEAD_PAD)
        q_ref[:, sl] = (_rope(q[:, sl], c, s_up, s_dn, shift) * scale).astype(BF16)

    ckv = (_rms(p[:, C_Q_RANK:C_Q_RANK + C_KV_RANK]) * gkv_ref[...]).astype(BF16)
    k_pe = _rope(p[:, C_Q_RANK + C_KV_RANK:], c, s_up, s_dn, shift)
    k = _dot(ckv, wk_ref[...])
    for j in range(C_HEADS):
        sl = slice(j * HEAD_PAD, (j + 1) * HEAD_PAD)
        k_ref[:, sl] = (k[:, sl] + k_pe).astype(BF16)
    v_ref[...] = _dot(ckv, wv_ref[...]).astype(BF16)


def _p1_odd(layer, xs, mods, g, w_in, g_q, g_kv, w_q, w_k, w_v, rope):
    width = C_HEADS * HEAD_PAD
    in_cols = C_Q_RANK + C_KV_RANK + HEAD_PAD
    out = jax.ShapeDtypeStruct((NT, width), BF16)
    return pl.pallas_call(
        _p1_odd_kernel,
        grid=(NTILES,),
        in_specs=[_tok_spec(D_MODEL), _mod_spec(layer, 0), _mod_spec(layer, 1), _full((1, D_MODEL)),
                  _full((D_MODEL, in_cols)), _full((1, C_Q_RANK)), _full((1, C_KV_RANK)),
                  _full((C_Q_RANK, width)), _full((C_KV_RANK, width)), _full((C_KV_RANK, width)),
                  _rope_spec(), _rope_spec(), _rope_spec()],
        out_specs=[_tok_spec(width)] * 3,
        out_shape=[out] * 3,
        compiler_params=_params(1),
        name="p1_odd",
    )(xs, mods, mods, g, w_in, g_q, g_kv, w_q, w_k, w_v, *rope)


def _mla_kernel(q_ref, k_ref, v_ref, o_ref, m_ref, l_ref, acc_ref):
    m_ref[...] = jnp.full(m_ref.shape, -jnp.inf, F32)
    l_ref[...] = jnp.zeros(l_ref.shape, F32)
    acc_ref[...] = jnp.zeros(acc_ref.shape, F32)

    def process(start, size):
        for h in range(C_HEADS_PER_STEP):
            sl = slice(h * HEAD_PAD, (h + 1) * HEAD_PAD)
            s = _dot_nt(q_ref[:, sl], k_ref[pl.ds(start, size), sl])
            _softmax_step(s, v_ref[pl.ds(start, size), sl], m_ref.at[h], l_ref.at[h], acc_ref.at[h])

    _for_each_kv_chunk(process)

    for h in range(C_HEADS_PER_STEP):
        o_ref[:, h * HEAD_PAD:(h + 1) * HEAD_PAD] = (acc_ref[h] / l_ref[h]).astype(BF16)


def _mla_attn(q, k, v):
    width = C_HEADS_PER_STEP * HEAD_PAD
    kv_spec = pl.BlockSpec((None, TOK, width), lambda b, h, i: (b, 0, h))
    q_spec = pl.BlockSpec((None, TM, width), lambda b, h, i: (b, i, h))
    return pl.pallas_call(
        _mla_kernel,
        grid=(BATCH, C_HEADS // C_HEADS_PER_STEP, TPB),
        in_specs=[q_spec, kv_spec, kv_spec],
        out_specs=q_spec,
        out_shape=jax.ShapeDtypeStruct((BATCH, TOK, C_HEADS * HEAD_PAD), BF16),
        scratch_shapes=[pltpu.VMEM((C_HEADS_PER_STEP, TM, LANES), F32)] * 3,
        compiler_params=_params(3),
        name="mla_attn",
    )(q, k, v)


def _post_odd_kernel(o_ref, x_ref, gate_ref, w_ref, gp_ref, out_ref):
    y = _dot(o_ref[...], w_ref[...])
    out_ref[...] = x_ref[...] + gate_ref[...] * (_rms(y) * gp_ref[...])


def _post_odd(layer, o, xs, mods, w_out, g_post):
    width = C_HEADS * HEAD_PAD
    return pl.pallas_call(
        _post_odd_kernel,
        grid=(NTILES,),
        in_specs=[_tok_spec(width), _tok_spec(D_MODEL), _mod_spec(layer, 2), _full((width, D_MODEL)),
                  _full((1, D_MODEL))],
        out_specs=_tok_spec(D_MODEL),
        out_shape=jax.ShapeDtypeStruct((NT, D_MODEL), F32),
        compiler_params=_params(1),
        name="post_odd",
    )(o, xs, mods, w_out, g_post)


def _ffn_kernel(x_ref, xp_ref, xn_ref, sh_ref, sc_ref, gate_ref, g_ref, gp_ref, wu_ref, cw_ref, cb_ref, wd_ref,
                o_ref, g_scr, act_scr):
    r = _tile_in_batch(pl.program_id(0))
    prev_ok = (r >= 2).astype(F32)
    next_ok = ((r >= 1) & (r <= TPB - 2)).astype(F32)
    x = x_ref[...]
    h = _pre_norm(x, g_ref, sc_ref, sh_ref)
    h_prev = _pre_norm(xp_ref[...], g_ref, sc_ref, sh_ref)
    h_next = _pre_norm(xn_ref[...], g_ref, sc_ref, sh_ref)
    hb = h.astype(BF16)
    hb_ext = jnp.concatenate([h_prev, h, h_next], axis=0).astype(BF16)
    lo, hi = F_HALO, F_HALO + TM
    for c in range(FFN_HIDDEN // FFN_CHUNK):
        cols = slice(c * FFN_CHUNK, (c + 1) * FFN_CHUNK)
        gate = _dot(hb_ext, wu_ref[:, FFN_HIDDEN + c * FFN_CHUNK:FFN_HIDDEN + (c + 1) * FFN_CHUNK])
        g_scr[0:lo, :] = gate[0:lo] * prev_ok
        g_scr[lo:hi, :] = gate[lo:hi]
        g_scr[hi:, :] = gate[hi:] * next_ok
        conv = (cw_ref[0:1, cols] * g_scr[lo - 1:hi - 1, :] + cw_ref[1:2, cols] * gate[lo:hi]
                + cw_ref[2:3, cols] * g_scr[lo + 1:hi + 1, :] + cb_ref[:, cols])
        val = _dot(hb, wu_ref[:, cols])
        act_scr[:, cols] = (_silu(conv) * val).astype(BF16)
    y = _dot(act_scr[...], wd_ref[...])
    o_ref[...] = x + gate_ref[...] * (_rms(y) * gp_ref[...])


def _ffn(layer, xs, mods, g_pre, g_post, w_up, conv_w, conv_b, w_down):
    blocks = TM // F_HALO
    return pl.pallas_call(
        _ffn_kernel,
        grid=(NTILES,),
        in_specs=[_tok_spec(D_MODEL),
                  pl.BlockSpec((F_HALO, D_MODEL), lambda t: (jnp.maximum(t * blocks - 1, 0), 0)),
                  pl.BlockSpec((F_HALO, D_MODEL), lambda t: (jnp.minimum((t + 1) * blocks, NT // F_HALO - 1), 0)),
                  _mod_spec(layer, 3), _mod_spec(layer, 4), _mod_spec(layer, 5),
                  _full((1, D_MODEL)), _full((1, D_MODEL)), _full((D_MODEL, 2 * FFN_HIDDEN)),
                  _full((SUBLANES, FFN_HIDDEN)), _full((1, FFN_HIDDEN)), _full((FFN_HIDDEN, D_MODEL))],
        out_specs=_tok_spec(D_MODEL),
        out_shape=jax.ShapeDtypeStruct((NT, D_MODEL), F32),
        scratch_shapes=[pltpu.VMEM((TM + 2 * F_HALO, FFN_CHUNK), F32), pltpu.VMEM((TM, FFN_HIDDEN), BF16)],
        compiler_params=_params(1),
        name="conv_ffn",
    )(xs, xs, xs, mods, mods, mods, g_pre, g_post, w_up, conv_w, conv_b, w_down)


def _pad_rows(w, rows):
    return jnp.concatenate([w, jnp.zeros((rows - w.shape[0],) + w.shape[1:], w.dtype)], axis=0)


def _pad_last(w, width):
    return jnp.concatenate([w, jnp.zeros(w.shape[:-1] + (width - w.shape[-1],), w.dtype)], axis=-1)


def _mla_weights(w_in, w_uq, w_ukv, w_out):
    rank = C_Q_RANK + C_KV_RANK
    pe = jnp.concatenate([jnp.zeros((D_MODEL, C_NOPE), F32), w_in[:, rank:],
                          jnp.zeros((D_MODEL, HEAD_PAD - C_NOPE - C_ROPE), F32)], axis=1)
    w_in_p = jnp.concatenate([w_in[:, :rank], pe], axis=1)
    w_q = _pad_last(w_uq.reshape(C_Q_RANK, C_HEADS, C_NOPE + C_ROPE), HEAD_PAD).reshape(C_Q_RANK, -1)
    kv = w_ukv.reshape(C_KV_RANK, C_HEADS, C_NOPE + C_V)
    w_k = _pad_last(kv[..., :C_NOPE], HEAD_PAD).reshape(C_KV_RANK, -1)
    w_v = _pad_last(kv[..., C_NOPE:], HEAD_PAD).reshape(C_KV_RANK, -1)
    w_o = _pad_last(w_out.reshape(C_HEADS, C_V, D_MODEL).transpose(0, 2, 1), HEAD_PAD).transpose(0, 2, 1)
    return tuple(w.astype(BF16) for w in (w_in_p, w_q, w_k, w_v, w_o.reshape(C_HEADS * HEAD_PAD, D_MODEL)))


def kernel(x, c, ctx, c_ctx, ada_w, ada_b, norm_mix_pre, norm_mix_post, norm_ffn_pre, norm_ffn_post, ffn_w_up,
           ffn_conv_w, ffn_conv_b, ffn_w_down, ab_w_in, a_conv_w, a_conv_b, a_ln_g, a_ln_b, b_lambda, b_subln,
           ab_w_out, c_w_in, c_q_norm, c_kv_norm, c_w_uq, c_w_ukv, c_w_out):
    mods = _modulation(c, c_ctx, ada_w, ada_b)
    xs = jnp.concatenate([ctx, x], axis=1).reshape(NT, D_MODEL)
    rope_b = _rope_tables(B_HEAD_DIM, 0, B_HEAD_DIM)
    rope_c = _rope_tables(C_ROPE, C_NOPE, HEAD_PAD)
    row = lambda v: v.reshape(1, -1)
    tokens = lambda t: t.reshape(BATCH, TOK, t.shape[-1])
    flat = lambda t: t.reshape(NT, t.shape[-1])

    for i in range(DEPTH):
        j = i // 2
        if i % 2 == 0:
            lambda_init = 0.8 - 0.6 * math.exp(-0.3 * i)
            a, q, k, v = _p1_even(i, xs, mods, row(norm_mix_pre[i]), ab_w_in[j].astype(BF16), rope_b)
            ob = _diff_attn(tokens(q), tokens(k), tokens(v), b_lambda[j], row(b_subln[j]), lambda_init)
            xs = _post_even(i, a, flat(ob), xs, mods, ab_w_out[j].astype(BF16),
                            _pad_rows(a_conv_w[j], A_KERNEL + 1), row(a_conv_b[j]), row(a_ln_g[j]),
                            row(a_ln_b[j]), row(norm_mix_post[i]))
        else:
            w_in, w_q, w_k, w_v, w_o = _mla_weights(c_w_in[j], c_w_uq[j], c_w_ukv[j], c_w_out[j])
            q, k, v = _p1_odd(i, xs, mods, row(norm_mix_pre[i]), w_in, row(c_q_norm[j]), row(c_kv_norm[j]),
                              w_q, w_k, w_v, rope_c)
            o = _mla_attn(tokens(q), tokens(k), tokens(v))
            xs = _post_odd(i, flat(o), xs, mods, w_o, row(norm_mix_post[i]))
        xs = _ffn(i, xs, mods, row(norm_ffn_pre[i]), row(norm_ffn_post[i]), ffn_w_up[i].astype(BF16),
                  _pad_rows(ffn_conv_w[i], SUBLANES), row(ffn_conv_b[i]), ffn_w_down[i].astype(BF16))
    return xs.reshape(BATCH, TOK, D_MODEL)[:, CTX_LEN:, :]
```

```python
import functools
import math

import jax
import jax.numpy as jnp
from jax import lax
from jax.experimental import pallas as pl
from jax.experimental.pallas import tpu as pltpu

D_MODEL = 1024
BATCH = 8
SEQ = 4096
DEPTH = 4
CTX_LEN = 256
GRID_W = 64
ROPE_BASE = 10000.0
RMS_EPS = 1e-6
LN_EPS = 1e-5

A_WIDTH = 512
A_KERNEL = 31
B_HEADS = 4
B_HEAD_DIM = 64
B_V_DIM = 2 * B_HEAD_DIM
B_QK_COLS = B_HEADS * 2 * B_HEAD_DIM
B_WIDTH = B_HEADS * B_V_DIM
AB_IN = 2 * A_WIDTH + 2 * B_QK_COLS + B_WIDTH

C_HEADS = 16
C_NOPE = 64
C_ROPE = 32
C_V = 64
C_Q_RANK = 768
C_KV_RANK = 256

FFN_HIDDEN = 2816
FFN_KERNEL = 3

LANES = 128
SUBLANES = 8
BF16_SUBLANES = 16
MXU_COLS = 256
VMEM_LIMIT_BYTES = 56 * 1024 * 1024

TOK = CTX_LEN + SEQ
NT = BATCH * TOK
TM = 256
TPB = TOK // TM
NTILES = NT // TM
MOD_ROWS = 16
CTX_ROW = BATCH
HEAD_PAD = LANES
C_HEADS_PER_STEP = 4
KV_CHUNK = 512
SCORE_LOOKAHEAD = 4
FFN_CHUNK = MXU_COLS
A_HALO = 16
F_HALO = SUBLANES

F32 = jnp.float32
BF16 = jnp.bfloat16
LOG2_E = math.log2(math.e)


def _dot(a, b):
    return jnp.dot(a, b, preferred_element_type=F32)


def _dot_nt(a, b):
    return lax.dot_general(a, b, (((1,), (1,)), ((), ())), preferred_element_type=F32)


def _rms(x, eps=RMS_EPS):
    return x * lax.rsqrt(jnp.mean(x * x, axis=-1, keepdims=True) + eps)


def _silu(x):
    return x * jax.nn.sigmoid(x)


def _params(n_axes):
    return pltpu.CompilerParams(dimension_semantics=("parallel",) * n_axes, vmem_limit_bytes=VMEM_LIMIT_BYTES)


def _full(shape):
    return pl.BlockSpec(shape, lambda *_: (0,) * len(shape))


def _tile_in_batch(t):
    return t % TPB


def _mod_spec(layer, slot):
    def index(t):
        row = jnp.where(_tile_in_batch(t) == 0, CTX_ROW, t // TPB)
        return ((layer * MOD_ROWS + row) * 6 + slot, 0, 0)
    return pl.BlockSpec((None, 1, D_MODEL), index)


def _tok_spec(width):
    return pl.BlockSpec((TM, width), lambda t: (t, 0))


def _rope_spec():
    return pl.BlockSpec((TM, LANES), lambda t: (_tile_in_batch(t), 0))


def _rope(y, c, s_up, s_dn, shift):
    return y * c + pltpu.roll(y, LANES - shift, 1) * s_up + pltpu.roll(y, shift, 1) * s_dn


def _rope_tables(group, lane_offset, period):
    half = group // 4
    lane = jnp.arange(LANES)
    rel = (lane % period) - lane_offset
    active = (rel >= 0) & (rel < group)
    rel = jnp.where(active, rel, 0)
    by_col = rel >= group // 2
    within = rel % (group // 2)
    freq_idx = within % half
    inv_freq = ROPE_BASE ** (-freq_idx.astype(F32) / half)
    t = jnp.arange(SEQ, dtype=jnp.int32)
    pos = jnp.where(by_col[None, :], (t % GRID_W)[:, None], (t // GRID_W)[:, None]).astype(F32)
    ang = pos * inv_freq[None, :]
    cos, sin = jnp.cos(ang), jnp.sin(ang)
    first = within < half
    c = jnp.where(active[None, :], cos, 1.0)
    s_up = jnp.where((active & first)[None, :], -sin, 0.0)
    s_dn = jnp.where((active & ~first)[None, :], sin, 0.0)
    ident = [jnp.ones((CTX_LEN, LANES), F32), jnp.zeros((CTX_LEN, LANES), F32), jnp.zeros((CTX_LEN, LANES), F32)]
    return tuple(jnp.concatenate([i, x.astype(F32)], axis=0) for i, x in zip(ident, (c, s_up, s_dn)))


def _mods_kernel(c_ref, w_ref, b_ref, o_ref):
    s = _silu(c_ref[...])
    o_ref[...] = jnp.dot(s, w_ref[...], preferred_element_type=F32, precision=lax.Precision.HIGHEST) + b_ref[...]


def _modulation(c, c_ctx, ada_w, ada_b):
    rows = jnp.concatenate([c, c_ctx[None, :], jnp.zeros((MOD_ROWS - BATCH - 1, D_MODEL), F32)], axis=0)
    n_blk = 1536
    out = pl.pallas_call(
        _mods_kernel,
        grid=(DEPTH, 6 * D_MODEL // n_blk),
        in_specs=[
            pl.BlockSpec((MOD_ROWS, D_MODEL), lambda i, n: (0, 0)),
            pl.BlockSpec((None, D_MODEL, n_blk), lambda i, n: (i, 0, n)),
            pl.BlockSpec((None, 1, n_blk), lambda i, n: (i, 0, n)),
        ],
        out_specs=pl.BlockSpec((None, MOD_ROWS, n_blk), lambda i, n: (i, 0, n)),
        out_shape=jax.ShapeDtypeStruct((DEPTH, MOD_ROWS, 6 * D_MODEL), F32),
        compiler_params=_params(2),
        name="modulation",
    )(rows, ada_w, ada_b.reshape(DEPTH, 1, 6 * D_MODEL))
    return out.reshape(DEPTH * MOD_ROWS * 6, 1, D_MODEL)


def _pre_norm(x, g_ref, sc_ref, sh_ref):
    return _rms(x) * (g_ref[...] * (1.0 + sc_ref[...])) + sh_ref[...]


def _p1_even_kernel(x_ref, sh_ref, sc_ref, g_ref, w_ref, rc_ref, ru_ref, rd_ref, a_ref, q_ref, k_ref, v_ref):
    hb = _pre_norm(x_ref[...], g_ref, sc_ref, sh_ref).astype(BF16)
    a_ref[...] = _dot(hb, w_ref[:, 0:2 * A_WIDTH]).astype(BF16)
    c, s_up, s_dn = rc_ref[...], ru_ref[...], rd_ref[...]
    o = 2 * A_WIDTH
    for col, out_ref, scale in ((o, q_ref, B_HEAD_DIM ** -0.5 * LOG2_E), (o + B_QK_COLS, k_ref, 1.0)):
        y = _dot(hb, w_ref[:, col:col + B_QK_COLS])
        for j in range(B_QK_COLS // LANES):
            yj = _rope(y[:, j * LANES:(j + 1) * LANES], c, s_up, s_dn, B_HEAD_DIM // 4)
            out_ref[:, j * LANES:(j + 1) * LANES] = (yj * scale).astype(BF16)
    v_ref[...] = _dot(hb, w_ref[:, o + 2 * B_QK_COLS:AB_IN]).astype(BF16)


def _p1_even(layer, xs, mods, g, w_in, rope):
    return pl.pallas_call(
        _p1_even_kernel,
        grid=(NTILES,),
        in_specs=[_tok_spec(D_MODEL), _mod_spec(layer, 0), _mod_spec(layer, 1), _full((1, D_MODEL)),
                  _full((D_MODEL, AB_IN)), _rope_spec(), _rope_spec(), _rope_spec()],
        out_specs=[_tok_spec(2 * A_WIDTH), _tok_spec(B_QK_COLS), _tok_spec(B_QK_COLS), _tok_spec(B_WIDTH)],
        out_shape=[jax.ShapeDtypeStruct((NT, 2 * A_WIDTH), BF16), jax.ShapeDtypeStruct((NT, B_QK_COLS), BF16),
                   jax.ShapeDtypeStruct((NT, B_QK_COLS), BF16), jax.ShapeDtypeStruct((NT, B_WIDTH), BF16)],
        compiler_params=_params(1),
        name="p1_even",
    )(xs, mods, mods, g, w_in, *rope)


def _softmax_step(st, vt, m_ref, acc_ref):
    m_prev = m_ref[...]
    m_new = jnp.maximum(m_prev, jnp.max(st, axis=0, keepdims=True))
    alpha = jnp.exp2(m_prev - m_new)
    p = jnp.exp2((st - m_new).astype(BF16))
    acc_ref[...] = alpha * acc_ref[...] + _dot(vt, p)
    m_ref[...] = m_new


def _attend(streams, scores, consume):
    ctx_only = [(0, CTX_LEN)]
    everything = ctx_only + [(CTX_LEN + i * KV_CHUNK, KV_CHUNK) for i in range(SEQ // KV_CHUNK)]
    qi = pl.program_id(2)
    for chunks, cond in ((ctx_only, qi == 0), (everything, qi > 0)):
        items = [(start, size, s) for start, size in chunks for s in range(streams)]

        @pl.when(cond)
        def _(items=items):
            pending = [scores(*item) for item in items[:SCORE_LOOKAHEAD]]
            for n, item in enumerate(items):
                if n + SCORE_LOOKAHEAD < len(items):
                    pending.append(scores(*items[n + SCORE_LOOKAHEAD]))
                consume(*item, pending.pop(0))


def _diff_attn_kernel(lam_ref, sub_ref, q_ref, k_ref, vt_ref, o_ref, m_ref, acc_ref, *, lambda_init):
    q = q_ref[...]
    lane = lax.broadcasted_iota(jnp.int32, q.shape, 1)
    zero = jnp.zeros_like(q)
    qs = (jnp.where(lane < B_HEAD_DIM, q, zero), jnp.where(lane >= B_HEAD_DIM, q, zero))
    m_ref[...] = jnp.full(m_ref.shape, -jnp.inf, F32)
    acc_ref[...] = jnp.zeros(acc_ref.shape, F32)

    def scores(start, size, i):
        return _dot_nt(k_ref[start:start + size, :], qs[i])

    def consume(start, size, i, st):
        _softmax_step(st, vt_ref[:, start:start + size], m_ref.at[i], acc_ref.at[i])

    _attend(2, scores, consume)

    lv = lam_ref[...]
    lam = (jnp.exp(jnp.sum(lv[0:1] * lv[1:2], axis=-1, keepdims=True))
           - jnp.exp(jnp.sum(lv[2:3] * lv[3:4], axis=-1, keepdims=True)) + lambda_init)
    o1 = acc_ref[0, 0:B_V_DIM, :] / acc_ref[0, B_V_DIM:B_V_DIM + 1, :]
    o2 = acc_ref[1, 0:B_V_DIM, :] / acc_ref[1, B_V_DIM:B_V_DIM + 1, :]
    o = (o1 - lam * o2).T
    o_ref[...] = (_rms(o) * (sub_ref[...] * (1.0 - lambda_init))).astype(BF16)


def _diff_attn(q, k, vt, lam_vecs, subln_g, lambda_init):
    rows = B_V_DIM + BF16_SUBLANES
    return pl.pallas_call(
        functools.partial(_diff_attn_kernel, lambda_init=lambda_init),
        grid=(BATCH, B_HEADS, TPB),
        in_specs=[pl.BlockSpec((4, B_HEAD_DIM), lambda b, h, i: (0, 0)),
                  pl.BlockSpec((1, B_V_DIM), lambda b, h, i: (0, 0)),
                  pl.BlockSpec((None, TM, LANES), lambda b, h, i: (b, i, h)),
                  pl.BlockSpec((None, TOK, LANES), lambda b, h, i: (b, 0, h)),
                  pl.BlockSpec((None, None, rows, TOK), lambda b, h, i: (b, h, 0, 0))],
        out_specs=pl.BlockSpec((None, TM, B_V_DIM), lambda b, h, i: (b, i, h)),
        out_shape=jax.ShapeDtypeStruct((BATCH, TOK, B_WIDTH), BF16),
        scratch_shapes=[pltpu.VMEM((2, 1, TM), F32), pltpu.VMEM((2, rows, TM), F32)],
        compiler_params=_params(3),
        name="diff_attn",
    )(lam_vecs, subln_g, q, k, vt)


def _values_t(v, heads, dim):
    vt = v.reshape(BATCH, TOK, heads, dim).transpose(0, 2, 3, 1)
    return jnp.concatenate([vt, jnp.ones((BATCH, heads, BF16_SUBLANES, TOK), v.dtype)], axis=2)


def _post_even_kernel(a_ref, ap_ref, an_ref, ob_ref, x_ref, gate_ref, w_ref, cw_ref, cb_ref, lg_ref, lb_ref,
                      gp_ref, o_ref, u_ref):
    r = _tile_in_batch(pl.program_id(0))
    prev_ok = (r >= 2).astype(F32)
    next_ok = ((r >= 1) & (r <= TPB - 2)).astype(F32)

    def glu(a):
        a = a.astype(F32)
        return a[:, :A_WIDTH] * jax.nn.sigmoid(a[:, A_WIDTH:])

    u_ref[0:A_HALO, :] = glu(ap_ref[...]) * prev_ok
    u_ref[A_HALO:A_HALO + TM, :] = glu(a_ref[...])
    u_ref[A_HALO + TM:, :] = glu(an_ref[...]) * next_ok

    rows = 32
    base = A_HALO - A_KERNEL // 2
    outs = []
    for r0 in range(0, TM, rows):
        acc = jnp.broadcast_to(cb_ref[...], (rows, A_WIDTH))
        for tap in range(A_KERNEL):
            acc = acc + cw_ref[tap:tap + 1, :] * u_ref[r0 + base + tap:r0 + base + tap + rows, :]
        mu = jnp.mean(acc, axis=-1, keepdims=True)
        cen = acc - mu
        var = jnp.mean(cen * cen, axis=-1, keepdims=True)
        y = cen * lax.rsqrt(var + LN_EPS) * lg_ref[...] + lb_ref[...]
        outs.append(_silu(y).astype(BF16))
    ya = jnp.concatenate(outs, axis=0)
    y = _dot(ya, w_ref[0:A_WIDTH, :]) + _dot(ob_ref[...], w_ref[A_WIDTH:, :])
    o_ref[...] = x_ref[...] + gate_ref[...] * (_rms(y) * gp_ref[...])


def _post_even(layer, a, ob, xs, mods, w_out, conv_w, conv_b, ln_g, ln_b, g_post):
    blocks = TM // A_HALO
    return pl.pallas_call(
        _post_even_kernel,
        grid=(NTILES,),
        in_specs=[_tok_spec(2 * A_WIDTH),
                  pl.BlockSpec((A_HALO, 2 * A_WIDTH), lambda t: (jnp.maximum(t * blocks - 1, 0), 0)),
                  pl.BlockSpec((A_HALO, 2 * A_WIDTH), lambda t: (jnp.minimum((t + 1) * blocks, NT // A_HALO - 1), 0)),
                  _tok_spec(B_WIDTH), _tok_spec(D_MODEL), _mod_spec(layer, 2),
                  _full((A_WIDTH + B_WIDTH, D_MODEL)), _full((A_KERNEL + 1, A_WIDTH)), _full((1, A_WIDTH)),
                  _full((1, A_WIDTH)), _full((1, A_WIDTH)), _full((1, D_MODEL))],
        out_specs=_tok_spec(D_MODEL),
        out_shape=jax.ShapeDtypeStruct((NT, D_MODEL), F32),
        scratch_shapes=[pltpu.VMEM((TM + 2 * A_HALO, A_WIDTH), F32)],
        compiler_params=_params(1),
        name="post_even",
    )(a, a, a, ob, xs, mods, w_out, conv_w, conv_b, ln_g, ln_b, g_post)


def _p1_odd_kernel(x_ref, sh_ref, sc_ref, g_ref, w_ref, gq_ref, gkv_ref, wq_ref, wk_ref, wv_ref,
                   rc_ref, ru_ref, rd_ref, q_ref, k_ref, v_ref):
    hb = _pre_norm(x_ref[...], g_ref, sc_ref, sh_ref).astype(BF16)
    p = _dot(hb, w_ref[...])
    c, s_up, s_dn = rc_ref[...], ru_ref[...], rd_ref[...]
    shift = C_ROPE // 4
    scale = (C_NOPE + C_ROPE) ** -0.5 * LOG2_E

    cq =(_rms(p[:, :C_Q_RANK]) * gq_ref[...]).astype(BF16)
    q = _dot(cq, wq_ref[...])
    for j in range(C_HEADS):
        sl = slice(j * HEAD_PAD, (j + 1) * HEAD_PAD)
        q_ref[:, sl] = (_rope(q[:, sl], c, s_up, s_dn, shift) * scale).astype(BF16)

    ckv = (_rms(p[:, C_Q_RANK:C_Q_RANK + C_KV_RANK]) * gkv_ref[...]).astype(BF16)
    k_pe = _rope(p[:, C_Q_RANK + C_KV_RANK:], c, s_up, s_dn, shift)
    k = _dot(ckv, wk_ref[...])
    for j in range(C_HEADS):
        sl = slice(j * HEAD_PAD, (j + 1) * HEAD_PAD)
        k_ref[:, sl] = (k[:, sl] + k_pe).astype(BF16)
    v_ref[...] = _dot(ckv, wv_ref[...]).astype(BF16)


def _p1_odd(layer, xs, mods, g, w_in, g_q, g_kv, w_q, w_k, w_v, rope):
    width = C_HEADS * HEAD_PAD
    v_width = C_HEADS * C_V
    in_cols = C_Q_RANK + C_KV_RANK + HEAD_PAD
    out = jax.ShapeDtypeStruct((NT, width), BF16)
    return pl.pallas_call(
        _p1_odd_kernel,
        grid=(NTILES,),
        in_specs=[_tok_spec(D_MODEL), _mod_spec(layer, 0), _mod_spec(layer, 1), _full((1, D_MODEL)),
                  _full((D_MODEL, in_cols)), _full((1, C_Q_RANK)), _full((1, C_KV_RANK)),
                  _full((C_Q_RANK, width)), _full((C_KV_RANK, width)), _full((C_KV_RANK, v_width)),
                  _rope_spec(), _rope_spec(), _rope_spec()],
        out_specs=[_tok_spec(width), _tok_spec(width), _tok_spec(v_width)],
        out_shape=[out, out, jax.ShapeDtypeStruct((NT, v_width), BF16)],
        compiler_params=_params(1),
        name="p1_odd",
    )(xs, mods, mods, g, w_in, g_q, g_kv, w_q, w_k, w_v, *rope)


def _mla_kernel(q_ref, k_ref, vt_ref, o_ref, m_ref, acc_ref):
    m_ref[...] = jnp.full(m_ref.shape, -jnp.inf, F32)
    acc_ref[...] = jnp.zeros(acc_ref.shape, F32)

    def scores(start, size, h):
        sl = slice(h * HEAD_PAD, (h + 1) * HEAD_PAD)
        return _dot_nt(k_ref[start:start + size, sl], q_ref[:, sl])

    def consume(start, size, h, st):
        _softmax_step(st, vt_ref[h, :, start:start + size], m_ref.at[h], acc_ref.at[h])

    _attend(C_HEADS_PER_STEP, scores, consume)

    o_t = [acc_ref[h, 0:C_V, :] / acc_ref[h, C_V:C_V + 1, :] for h in range(C_HEADS_PER_STEP)]
    o_ref[...] = jnp.concatenate(o_t, axis=0).T.astype(BF16)


def _mla_attn(q, k, vt):
    width = C_HEADS_PER_STEP * HEAD_PAD
    rows = C_V + BF16_SUBLANES
    return pl.pallas_call(
        _mla_kernel,
        grid=(BATCH, C_HEADS // C_HEADS_PER_STEP, TPB),
        in_specs=[pl.BlockSpec((None, TM, width), lambda b, h, i: (b, i, h)),
                  pl.BlockSpec((None, TOK, width), lambda b, h, i: (b, 0, h)),
                  pl.BlockSpec((None, C_HEADS_PER_STEP, rows, TOK), lambda b, h, i: (b, h, 0, 0))],
        out_specs=pl.BlockSpec((None, TM, C_HEADS_PER_STEP * C_V), lambda b, h, i: (b, i, h)),
        out_shape=jax.ShapeDtypeStruct((BATCH, TOK, C_HEADS * C_V), BF16),
        scratch_shapes=[pltpu.VMEM((C_HEADS_PER_STEP, 1, TM), F32), pltpu.VMEM((C_HEADS_PER_STEP, rows, TM), F32)],
        compiler_params=_params(3),
        name="mla_attn",
    )(q, k, vt)


def _post_odd_kernel(o_ref, x_ref, gate_ref, w_ref, gp_ref, out_ref):
    y = _dot(o_ref[...], w_ref[...])
    out_ref[...] = x_ref[...] + gate_ref[...] * (_rms(y) * gp_ref[...])


def _post_odd(layer, o, xs, mods, w_out, g_post):
    width = C_HEADS * C_V
    return pl.pallas_call(
        _post_odd_kernel,
        grid=(NTILES,),
        in_specs=[_tok_spec(width), _tok_spec(D_MODEL), _mod_spec(layer, 2), _full((width, D_MODEL)),
                  _full((1, D_MODEL))],
        out_specs=_tok_spec(D_MODEL),
        out_shape=jax.ShapeDtypeStruct((NT, D_MODEL), F32),
        compiler_params=_params(1),
        name="post_odd",
    )(o, xs, mods, w_out, g_post)


def _ffn_kernel(x_ref, xp_ref, xn_ref, sh_ref, sc_ref, gate_ref, g_ref, gp_ref, wu_ref, cw_ref, cb_ref, wd_ref,
                o_ref, g_scr, act_scr):
    r = _tile_in_batch(pl.program_id(0))
    prev_ok = (r >= 2).astype(F32)
    next_ok = ((r >= 1) & (r <= TPB - 2)).astype(F32)
    x = x_ref[...]
    h = _pre_norm(x, g_ref, sc_ref, sh_ref)
    h_prev = _pre_norm(xp_ref[...], g_ref, sc_ref, sh_ref)
    h_next = _pre_norm(xn_ref[...], g_ref, sc_ref, sh_ref)
    hb = h.astype(BF16)
    hb_ext = jnp.concatenate([h_prev, h, h_next], axis=0).astype(BF16)
    lo, hi = F_HALO, F_HALO + TM
    for c in range(FFN_HIDDEN // FFN_CHUNK):
        cols = slice(c * FFN_CHUNK, (c + 1) * FFN_CHUNK)
        gate = _dot(hb_ext, wu_ref[:, FFN_HIDDEN + c * FFN_CHUNK:FFN_HIDDEN + (c + 1) * FFN_CHUNK])
        g_scr[0:lo, :] = gate[0:lo] * prev_ok
        g_scr[lo:hi, :] = gate[lo:hi]
        g_scr[hi:, :] = gate[hi:] * next_ok
        conv = (cw_ref[0:1, cols] * g_scr[lo - 1:hi - 1, :] + cw_ref[1:2, cols] * gate[lo:hi]
                + cw_ref[2:3, cols] * g_scr[lo + 1:hi + 1, :] + cb_ref[:, cols])
        val = _dot(hb, wu_ref[:, cols])
        act_scr[:, cols] = (_silu(conv) * val).astype(BF16)
    y = _dot(act_scr[...], wd_ref[...])
    o_ref[...] = x + gate_ref[...] * (_rms(y) * gp_ref[...])


def _ffn(layer, xs, mods, g_pre, g_post, w_up, conv_w, conv_b, w_down):
    blocks = TM // F_HALO
    return pl.pallas_call(
        _ffn_kernel,
        grid=(NTILES,),
        in_specs=[_tok_spec(D_MODEL),
                  pl.BlockSpec((F_HALO, D_MODEL), lambda t: (jnp.maximum(t * blocks - 1, 0), 0)),
                  pl.BlockSpec((F_HALO, D_MODEL), lambda t: (jnp.minimum((t + 1) * blocks, NT // F_HALO - 1), 0)),
                  _mod_spec(layer, 3), _mod_spec(layer, 4), _mod_spec(layer, 5),
                  _full((1, D_MODEL)), _full((1, D_MODEL)), _full((D_MODEL, 2 * FFN_HIDDEN)),
                  _full((SUBLANES, FFN_HIDDEN)), _full((1, FFN_HIDDEN)), _full((FFN_HIDDEN, D_MODEL))],
        out_specs=_tok_spec(D_MODEL),
        out_shape=jax.ShapeDtypeStruct((NT, D_MODEL), F32),
        scratch_shapes=[pltpu.VMEM((TM + 2 * F_HALO, FFN_CHUNK), F32), pltpu.VMEM((TM, FFN_HIDDEN), BF16)],
        compiler_params=_params(1),
        name="conv_ffn",
    )(xs, xs, xs, mods, mods, mods, g_pre, g_post, w_up, conv_w, conv_b, w_down)


def _pad_rows(w, rows):
    return jnp.concatenate([w, jnp.zeros((rows - w.shape[0],) + w.shape[1:], w.dtype)], axis=0)


def _pad_last(w, width):
    return jnp.concatenate([w, jnp.zeros(w.shape[:-1] + (width - w.shape[-1],), w.dtype)], axis=-1)


def _mla_weights(w_in, w_uq, w_ukv, w_out):
    rank = C_Q_RANK + C_KV_RANK
    pe = jnp.concatenate([jnp.zeros((D_MODEL, C_NOPE), F32), w_in[:, rank:],
                          jnp.zeros((D_MODEL, HEAD_PAD - C_NOPE - C_ROPE), F32)], axis=1)
    w_in_p = jnp.concatenate([w_in[:, :rank], pe], axis=1)
    w_q = _pad_last(w_uq.reshape(C_Q_RANK, C_HEADS, C_NOPE + C_ROPE), HEAD_PAD).reshape(C_Q_RANK, -1)
    kv = w_ukv.reshape(C_KV_RANK, C_HEADS, C_NOPE + C_V)
    w_k = _pad_last(kv[..., :C_NOPE], HEAD_PAD).reshape(C_KV_RANK, -1)
    w_v = kv[..., C_NOPE:].reshape(C_KV_RANK, -1)
    return tuple(w.astype(BF16) for w in (w_in_p, w_q, w_k, w_v, w_out))


def kernel(x, c, ctx, c_ctx, ada_w, ada_b, norm_mix_pre, norm_mix_post, norm_ffn_pre, norm_ffn_post, ffn_w_up,
           ffn_conv_w, ffn_conv_b, ffn_w_down, ab_w_in, a_conv_w, a_conv_b, a_ln_g, a_ln_b, b_lambda, b_subln,
           ab_w_out, c_w_in, c_q_norm, c_kv_norm, c_w_uq, c_w_ukv, c_w_out):
    mods = _modulation(c, c_ctx, ada_w, ada_b)
    xs = jnp.concatenate([ctx, x], axis=1).reshape(NT, D_MODEL)
    rope_b = _rope_tables(B_HEAD_DIM, 0, B_HEAD_DIM)
    rope_c = _rope_tables(C_ROPE, C_NOPE, HEAD_PAD)
    row = lambda v: v.reshape(1, -1)
    tokens = lambda t: t.reshape(BATCH, TOK, t.shape[-1])
    flat = lambda t: t.reshape(NT, t.shape[-1])

    for i in range(DEPTH):
        j = i // 2
        if i % 2 == 0:
            lambda_init = 0.8 - 0.6 * math.exp(-0.3 * i)
            a, q, k, v = _p1_even(i, xs, mods, row(norm_mix_pre[i]), ab_w_in[j].astype(BF16), rope_b)
            ob = _diff_attn(tokens(q), tokens(k), _values_t(v, B_HEADS, B_V_DIM), b_lambda[j], row(b_subln[j]),
                            lambda_init)
            xs = _post_even(i, a, flat(ob), xs, mods, ab_w_out[j].astype(BF16),
                            _pad_rows(a_conv_w[j], A_KERNEL + 1), row(a_conv_b[j]), row(a_ln_g[j]),
                            row(a_ln_b[j]), row(norm_mix_post[i]))
        else:
            w_in, w_q, w_k, w_v, w_o = _mla_weights(c_w_in[j], c_w_uq[j], c_w_ukv[j], c_w_out[j])
            q, k, v = _p1_odd(i, xs, mods, row(norm_mix_pre[i]), w_in, row(c_q_norm[j]), row(c_kv_norm[j]),
                              w_q, w_k, w_v, rope_c)
            o = _mla_attn(tokens(q), tokens(k), _values_t(v, C_HEADS, C_V))
            xs = _post_odd(i, flat(o), xs, mods, w_o, row(norm_mix_post[i]))
        xs = _ffn(i, xs, mods, row(norm_ffn_pre[i]), row(norm_ffn_post[i]), ffn_w_up[i].astype(BF16),
                  _pad_rows(ffn_conv_w[i], SUBLANES), row(ffn_conv_b[i]), ffn_w_down[i].astype(BF16))
    return xs.reshape(BATCH, TOK, D_MODEL)[:, CTX_LEN:, :]
```

```python
import functools
import math
from typing import NamedTuple

import jax
import jax.numpy as jnp
from jax import lax
from jax.experimental import pallas as pl
from jax.experimental.pallas import tpu as pltpu

D_MODEL = 1024
BATCH = 8
SEQ = 4096
DEPTH = 4
CTX_LEN = 256
GRID_W = 64
ROPE_BASE = 10000.0
RMS_EPS = 1e-6
LN_EPS = 1e-5

A_WIDTH = 512
A_KERNEL = 31
B_HEADS = 4
B_HEAD_DIM = 64
B_V_DIM = 2 * B_HEAD_DIM
B_QK_COLS = B_HEADS * 2 * B_HEAD_DIM
B_WIDTH = B_HEADS * B_V_DIM
AB_IN = 2 * A_WIDTH + 2 * B_QK_COLS + B_WIDTH

C_HEADS = 16
C_NOPE = 64
C_ROPE = 32
C_V = 64
C_Q_RANK = 768
C_KV_RANK = 256

FFN_HIDDEN = 2816
FFN_KERNEL = 3

LANES = 128
SUBLANES = 8
BF16_SUBLANES = 16
MXU_COLS = 256
VMEM_LIMIT_BYTES = 56 * 1024 * 1024

TOK = CTX_LEN + SEQ
NT = BATCH * TOK
TM = 256
TPB = TOK // TM
NTILES = NT // TM
MOD_ROWS = 16
CTX_ROW = BATCH
HEAD_PAD = LANES
C_HEADS_PER_STEP = 4
KV_CHUNK = 512
SCORE_LOOKAHEAD = 4
FFN_CHUNK = MXU_COLS
A_HALO = 16
F_HALO = SUBLANES

F32 = jnp.float32
BF16 = jnp.bfloat16
LOG2_E = math.log2(math.e)


def _dot(a, b):
    return jnp.dot(a, b, preferred_element_type=F32)


def _dot_nt(a, b):
    return lax.dot_general(a, b, (((1,), (1,)), ((), ())), preferred_element_type=F32)


def _rms(x, eps=RMS_EPS):
    return x * lax.rsqrt(jnp.mean(x * x, axis=-1, keepdims=True) + eps)


def _silu(x):
    return x * jax.nn.sigmoid(x)


def _params(n_axes):
    return pltpu.CompilerParams(dimension_semantics=("parallel",) * n_axes, vmem_limit_bytes=VMEM_LIMIT_BYTES)


def _full(shape):
    return pl.BlockSpec(shape, lambda *_: (0,) * len(shape))


def _tile_in_batch(t):
    return t % TPB


class _Stream(NamedTuple):
    tiles_per_batch: int
    has_ctx: bool

    @property
    def tiles(self):
        return BATCH * self.tiles_per_batch

    @property
    def rows(self):
        return self.tiles * TM


UNIFIED = _Stream(TPB, True)
LATENT = _Stream(TPB - 1, False)
assert DEPTH % 2 == 0, "only the odd-layer tail has a latent-only variant"


def _mod_spec(layer, slot, stream=UNIFIED):
    def index(t):
        row = t // stream.tiles_per_batch
        if stream.has_ctx:
            row = jnp.where(t % stream.tiles_per_batch == 0, CTX_ROW, row)
        return ((layer * MOD_ROWS + row) * 6 + slot, 0, 0)
    return pl.BlockSpec((None, 1, D_MODEL), index)


def _tok_spec(width):
    return pl.BlockSpec((TM, width), lambda t: (t, 0))


def _latent_of_unified_spec(width):
    return pl.BlockSpec((TM, width), lambda t: ((t // LATENT.tiles_per_batch) * TPB + t % LATENT.tiles_per_batch + 1, 0))


def _halo_specs(width, rows, stream=UNIFIED):
    blocks = TM // rows
    prev = lambda t: (jnp.maximum(t * blocks - 1, 0), 0)
    nxt = lambda t: (jnp.minimum((t + 1) * blocks, stream.rows // rows - 1), 0)
    return [pl.BlockSpec((rows, width), prev), pl.BlockSpec((rows, width), nxt)]


def _halo_valid(stream):
    r = pl.program_id(0) % stream.tiles_per_batch
    first_latent = 1 if stream.has_ctx else 0
    return ((r > first_latent).astype(F32),
            ((r >= first_latent) & (r < stream.tiles_per_batch - 1)).astype(F32))


def _vt_spec(heads, rows):
    return pl.BlockSpec((None, heads, rows, TM), lambda t: (t // TPB, 0, 0, _tile_in_batch(t)))


def _store_vt(vt_ref, vt, heads, dim):
    for h in range(heads):
        vt_ref[h, 0:dim, :] = vt[h * dim:(h + 1) * dim, :].astype(BF16)
        vt_ref[h, dim:dim + BF16_SUBLANES, :] = jnp.ones((BF16_SUBLANES, TM), BF16)


def _rope_spec():
    return pl.BlockSpec((TM, LANES), lambda t: (_tile_in_batch(t), 0))


def _rope(y, c, s_up, s_dn, shift):
    return y * c + pltpu.roll(y, LANES - shift, 1) * s_up + pltpu.roll(y, shift, 1) * s_dn


def _rope_tables(group, lane_offset, period):
    half = group // 4
    lane = jnp.arange(LANES)
    rel = (lane % period) - lane_offset
    active = (rel >= 0) & (rel < group)
    rel = jnp.where(active, rel, 0)
    by_col = rel >= group // 2
    within = rel % (group // 2)
    freq_idx = within % half
    inv_freq = ROPE_BASE ** (-freq_idx.astype(F32) / half)
    t = jnp.arange(SEQ, dtype=jnp.int32)
    pos = jnp.where(by_col[None, :], (t % GRID_W)[:, None], (t // GRID_W)[:, None]).astype(F32)
    ang = pos * inv_freq[None, :]
    cos, sin = jnp.cos(ang), jnp.sin(ang)
    first = within < half
    c = jnp.where(active[None, :], cos, 1.0)
    s_up = jnp.where((active & first)[None, :], -sin, 0.0)
    s_dn = jnp.where((active & ~first)[None, :], sin, 0.0)
    ident = [jnp.ones((CTX_LEN, LANES), F32), jnp.zeros((CTX_LEN, LANES), F32), jnp.zeros((CTX_LEN, LANES), F32)]
    return tuple(jnp.concatenate([i, x.astype(F32)], axis=0) for i, x in zip(ident, (c, s_up, s_dn)))


def _mods_kernel(c_ref, w_ref, b_ref, o_ref):
    s = _silu(c_ref[...])
    o_ref[...] = jnp.dot(s, w_ref[...], preferred_element_type=F32, precision=lax.Precision.HIGHEST) + b_ref[...]


def _modulation(c, c_ctx, ada_w, ada_b):
    rows = jnp.concatenate([c, c_ctx[None, :], jnp.zeros((MOD_ROWS - BATCH - 1, D_MODEL), F32)], axis=0)
    n_blk = 1536
    out = pl.pallas_call(
        _mods_kernel,
        grid=(DEPTH, 6 * D_MODEL // n_blk),
        in_specs=[
            pl.BlockSpec((MOD_ROWS, D_MODEL), lambda i, n: (0, 0)),
            pl.BlockSpec((None, D_MODEL, n_blk), lambda i, n: (i, 0, n)),
            pl.BlockSpec((None, 1, n_blk), lambda i, n: (i, 0, n)),
        ],
        out_specs=pl.BlockSpec((None, MOD_ROWS, n_blk), lambda i, n: (i, 0, n)),
        out_shape=jax.ShapeDtypeStruct((DEPTH, MOD_ROWS, 6 * D_MODEL), F32),
        compiler_params=_params(2),
        name="modulation",
    )(rows, ada_w, ada_b.reshape(DEPTH, 1, 6 * D_MODEL))
    return out.reshape(DEPTH * MOD_ROWS * 6, 1, D_MODEL)


def _pre_norm(x, g_ref, sc_ref, sh_ref):
    return _rms(x) * (g_ref[...] * (1.0 + sc_ref[...])) + sh_ref[...]


def _p1_even_kernel(x_ref, sh_ref, sc_ref, g_ref, w_ref, wvt_ref, rc_ref, ru_ref, rd_ref, a_ref, q_ref, k_ref, vt_ref):
    hb = _pre_norm(x_ref[...], g_ref, sc_ref, sh_ref).astype(BF16)
    a_ref[...] = _dot(hb, w_ref[:, 0:2 * A_WIDTH]).astype(BF16)
    c, s_up, s_dn = rc_ref[...], ru_ref[...], rd_ref[...]
    o = 2 * A_WIDTH
    for col, out_ref, scale in ((o, q_ref, B_HEAD_DIM ** -0.5 * LOG2_E), (o + B_QK_COLS, k_ref, 1.0)):
        y = _dot(hb, w_ref[:, col:col + B_QK_COLS])
        for j in range(B_QK_COLS // LANES):
            yj = _rope(y[:, j * LANES:(j + 1) * LANES], c, s_up, s_dn, B_HEAD_DIM // 4)
            out_ref[:, j * LANES:(j + 1) * LANES] = (yj * scale).astype(BF16)
    _store_vt(vt_ref, _dot_nt(wvt_ref[...], hb), B_HEADS, B_V_DIM)


def _p1_even(layer, xs, mods, g, w_in, rope):
    qk_cols = 2 * A_WIDTH + 2 * B_QK_COLS
    vt_rows = B_V_DIM + BF16_SUBLANES
    return pl.pallas_call(
        _p1_even_kernel,
        grid=(NTILES,),
        in_specs=[_tok_spec(D_MODEL), _mod_spec(layer, 0), _mod_spec(layer, 1), _full((1, D_MODEL)),
                  _full((D_MODEL, qk_cols)), _full((B_WIDTH, D_MODEL)), _rope_spec(), _rope_spec(), _rope_spec()],
        out_specs=[_tok_spec(2 * A_WIDTH), _tok_spec(B_QK_COLS), _tok_spec(B_QK_COLS), _vt_spec(B_HEADS, vt_rows)],
        out_shape=[jax.ShapeDtypeStruct((NT, 2 * A_WIDTH), BF16), jax.ShapeDtypeStruct((NT, B_QK_COLS), BF16),
                   jax.ShapeDtypeStruct((NT, B_QK_COLS), BF16),
                   jax.ShapeDtypeStruct((BATCH, B_HEADS, vt_rows, TOK), BF16)],
        compiler_params=_params(1),
        name="p1_even",
    )(xs, mods, mods, g, w_in[:, :qk_cols].astype(BF16), w_in[:, qk_cols:].T.astype(BF16), *rope)


def _softmax_step(st, vt, m_ref, acc_ref):
    m_prev = m_ref[...]
    m_new = jnp.maximum(m_prev, jnp.max(st, axis=0, keepdims=True))
    alpha = jnp.exp2(m_prev - m_new)
    p = jnp.exp2((st - m_new).astype(BF16))
    acc_ref[...] = alpha * acc_ref[...] + _dot(vt, p)
    m_ref[...] = m_new


def _attend(streams, scores, consume):
    ctx_only = [(0, CTX_LEN)]
    everything = ctx_only + [(CTX_LEN + i * KV_CHUNK, KV_CHUNK) for i in range(SEQ // KV_CHUNK)]
    qi = pl.program_id(2)
    for chunks, cond in ((ctx_only, qi == 0), (everything, qi > 0)):
        items = [(start, size, s) for start, size in chunks for s in range(streams)]

        @pl.when(cond)
        def _(items=items):
            pending = [scores(*item) for item in items[:SCORE_LOOKAHEAD]]
            for n, item in enumerate(items):
                if n + SCORE_LOOKAHEAD < len(items):
                    pending.append(scores(*items[n + SCORE_LOOKAHEAD]))
                consume(*item, pending.pop(0))


def _diff_attn_kernel(lam_ref, sub_ref, q_ref, k_ref, vt_ref, o_ref, m_ref, acc_ref, *, lambda_init):
    q = q_ref[...]
    lane = lax.broadcasted_iota(jnp.int32, q.shape, 1)
    zero = jnp.zeros_like(q)
    qs = (jnp.where(lane < B_HEAD_DIM, q, zero), jnp.where(lane >= B_HEAD_DIM, q, zero))
    m_ref[...] = jnp.full(m_ref.shape, -jnp.inf, F32)
    acc_ref[...] = jnp.zeros(acc_ref.shape, F32)

    def scores(start, size, i):
        return _dot_nt(k_ref[start:start + size, :], qs[i])

    def consume(start, size, i, st):
        _softmax_step(st, vt_ref[:, start:start + size], m_ref.at[i], acc_ref.at[i])

    _attend(2, scores, consume)

    lv = lam_ref[...]
    lam = (jnp.exp(jnp.sum(lv[0:1] * lv[1:2], axis=-1, keepdims=True))
           - jnp.exp(jnp.sum(lv[2:3] * lv[3:4], axis=-1, keepdims=True)) + lambda_init)
    o1 = acc_ref[0, 0:B_V_DIM, :] / acc_ref[0, B_V_DIM:B_V_DIM + 1, :]
    o2 = acc_ref[1, 0:B_V_DIM, :] / acc_ref[1, B_V_DIM:B_V_DIM + 1, :]
    o = (o1 - lam * o2).T
    o_ref[...] = (_rms(o) * (sub_ref[...] * (1.0 - lambda_init))).astype(BF16)


def _diff_attn(q, k, vt, lam_vecs, subln_g, lambda_init):
    rows = B_V_DIM + BF16_SUBLANES
    return pl.pallas_call(
        functools.partial(_diff_attn_kernel, lambda_init=lambda_init),
        grid=(BATCH, B_HEADS, TPB),
        in_specs=[pl.BlockSpec((4, B_HEAD_DIM), lambda b, h, i: (0, 0)),
                  pl.BlockSpec((1, B_V_DIM), lambda b, h, i: (0, 0)),
                  pl.BlockSpec((None, TM, LANES), lambda b, h, i: (b, i, h)),
                  pl.BlockSpec((None, TOK, LANES), lambda b, h, i: (b, 0, h)),
                  pl.BlockSpec((None, None, rows, TOK), lambda b, h, i: (b, h, 0, 0))],
        out_specs=pl.BlockSpec((None, TM, B_V_DIM), lambda b, h, i: (b, i, h)),
        out_shape=jax.ShapeDtypeStruct((BATCH, TOK, B_WIDTH), BF16),
        scratch_shapes=[pltpu.VMEM((2, 1, TM), F32), pltpu.VMEM((2, rows, TM), F32)],
        compiler_params=_params(3),
        name="diff_attn",
    )(lam_vecs, subln_g, q, k, vt)


def _post_even_kernel(a_ref, ap_ref, an_ref, ob_ref, x_ref, gate_ref, w_ref, cw_ref, cb_ref, lg_ref, lb_ref,
                      gp_ref, o_ref, u_ref):
    prev_ok, next_ok = _halo_valid(UNIFIED)

    def glu(a):
        a = a.astype(F32)
        return a[:, :A_WIDTH] * jax.nn.sigmoid(a[:, A_WIDTH:])

    ext = TM + 2 * A_HALO
    u_ref[0, 0:A_HALO, :] = glu(ap_ref[...]) * prev_ok
    u_ref[0, A_HALO:A_HALO + TM, :] = glu(a_ref[...])
    u_ref[0, A_HALO + TM:, :] = glu(an_ref[...]) * next_ok
    for s in range(1, SUBLANES):
        u_ref[s, 0:ext - SUBLANES, :] = u_ref[0, s:s + ext - SUBLANES, :]

    rows = 32
    base = A_HALO - A_KERNEL // 2
    outs = []
    for r0 in range(0, TM, rows):
        acc = jnp.broadcast_to(cb_ref[...], (rows, A_WIDTH))
        for tap in range(A_KERNEL):
            s = (base + tap) % SUBLANES
            i0 = r0 + base + tap - s
            acc = acc + cw_ref[tap:tap + 1, :] * u_ref[s, i0:i0 + rows, :]
        mu = jnp.mean(acc, axis=-1, keepdims=True)
        cen = acc - mu
        var = jnp.mean(cen * cen, axis=-1, keepdims=True)
        y = cen * lax.rsqrt(var + LN_EPS) * lg_ref[...] + lb_ref[...]
        outs.append(_silu(y).astype(BF16))
    ya = jnp.concatenate(outs, axis=0)
    y = _dot(ya, w_ref[0:A_WIDTH, :]) + _dot(ob_ref[...], w_ref[A_WIDTH:, :])
    o_ref[...] = x_ref[...] + gate_ref[...] * (_rms(y) * gp_ref[...])


def _post_even(layer, a, ob, xs, mods, w_out, conv_w, conv_b, ln_g, ln_b, g_post):
    return pl.pallas_call(
        _post_even_kernel,
        grid=(NTILES,),
        in_specs=[_tok_spec(2 * A_WIDTH), *_halo_specs(2 * A_WIDTH, A_HALO),
                  _tok_spec(B_WIDTH), _tok_spec(D_MODEL), _mod_spec(layer, 2),
                  _full((A_WIDTH + B_WIDTH, D_MODEL)), _full((A_KERNEL + 1, A_WIDTH)), _full((1, A_WIDTH)),
                  _full((1, A_WIDTH)), _full((1, A_WIDTH)), _full((1, D_MODEL))],
        out_specs=_tok_spec(D_MODEL),
        out_shape=jax.ShapeDtypeStruct((NT, D_MODEL), F32),
        scratch_shapes=[pltpu.VMEM((SUBLANES, TM + 2 * A_HALO, A_WIDTH), F32)],
        compiler_params=_params(1),
        name="post_even",
    )(a, a, a, ob, xs, mods, w_out, conv_w, conv_b, ln_g, ln_b, g_post)


def _p1_odd_kernel(x_ref, sh_ref, sc_ref, g_ref, w_ref, gq_ref, gkv_ref, wq_ref, wk_ref, wvt_ref,
                   rc_ref, ru_ref, rd_ref, q_ref, k_ref, vt_ref):
    hb = _pre_norm(x_ref[...], g_ref, sc_ref, sh_ref).astype(BF16)
    p = _dot(hb, w_ref[...])
    c, s_up, s_dn = rc_ref[...], ru_ref[...], rd_ref[...]
    shift = C_ROPE // 4
    scale = (C_NOPE + C_ROPE) ** -0.5 * LOG2_E

    cq =(_rms(p[:, :C_Q_RANK]) * gq_ref[...]).astype(BF16)
    q = _dot(cq, wq_ref[...])
    for j in range(C_HEADS):
        sl = slice(j * HEAD_PAD, (j + 1) * HEAD_PAD)
        q_ref[:, sl] = (_rope(q[:, sl], c, s_up, s_dn, shift) * scale).astype(BF16)

    ckv = (_rms(p[:, C_Q_RANK:C_Q_RANK + C_KV_RANK]) * gkv_ref[...]).astype(BF16)
    k_pe = _rope(p[:, C_Q_RANK + C_KV_RANK:], c, s_up, s_dn, shift)
    k = _dot(ckv, wk_ref[...])
    for j in range(C_HEADS):
        sl = slice(j * HEAD_PAD, (j + 1) * HEAD_PAD)
        k_ref[:, sl] = (k[:, sl] + k_pe).astype(BF16)
    _store_vt(vt_ref, _dot_nt(wvt_ref[...], ckv), C_HEADS, C_V)


def _p1_odd(layer, xs, mods, g, w_in, g_q, g_kv, w_q, w_k, w_vt, rope):
    width = C_HEADS * HEAD_PAD
    v_width = C_HEADS * C_V
    vt_rows = C_V + BF16_SUBLANES
    in_cols = C_Q_RANK + C_KV_RANK + HEAD_PAD
    out = jax.ShapeDtypeStruct((NT, width), BF16)
    return pl.pallas_call(
        _p1_odd_kernel,
        grid=(NTILES,),
        in_specs=[_tok_spec(D_MODEL), _mod_spec(layer, 0), _mod_spec(layer, 1), _full((1, D_MODEL)),
                  _full((D_MODEL, in_cols)), _full((1, C_Q_RANK)), _full((1, C_KV_RANK)),
                  _full((C_Q_RANK, width)), _full((C_KV_RANK, width)), _full((v_width, C_KV_RANK)),
                  _rope_spec(), _rope_spec(), _rope_spec()],
        out_specs=[_tok_spec(width), _tok_spec(width), _vt_spec(C_HEADS, vt_rows)],
        out_shape=[out, out, jax.ShapeDtypeStruct((BATCH, C_HEADS, vt_rows, TOK), BF16)],
        compiler_params=_params(1),
        name="p1_odd",
    )(xs, mods, mods, g, w_in, g_q, g_kv, w_q, w_k, w_vt, *rope)


def _mla_kernel(q_ref, k_ref, vt_ref, o_ref, m_ref, acc_ref):
    m_ref[...] = jnp.full(m_ref.shape, -jnp.inf, F32)
    acc_ref[...] = jnp.zeros(acc_ref.shape, F32)

    def scores(start, size, h):
        sl = slice(h * HEAD_PAD, (h + 1) * HEAD_PAD)
        return _dot_nt(k_ref[start:start + size, sl], q_ref[:, sl])

    def consume(start, size, h, st):
        _softmax_step(st, vt_ref[h, :, start:start + size], m_ref.at[h], acc_ref.at[h])

    _attend(C_HEADS_PER_STEP, scores, consume)

    o_t = [acc_ref[h, 0:C_V, :] / acc_ref[h, C_V:C_V + 1, :] for h in range(C_HEADS_PER_STEP)]
    o_ref[...] = jnp.concatenate(o_t, axis=0).T.astype(BF16)


def _mla_attn(q, k, vt):
    width = C_HEADS_PER_STEP * HEAD_PAD
    rows = C_V + BF16_SUBLANES
    return pl.pallas_call(
        _mla_kernel,
        grid=(BATCH, C_HEADS // C_HEADS_PER_STEP, TPB),
        in_specs=[pl.BlockSpec((None, TM, width), lambda b, h, i: (b, i, h)),
                  pl.BlockSpec((None, TOK, width), lambda b, h, i: (b, 0, h)),
                  pl.BlockSpec((None, C_HEADS_PER_STEP, rows, TOK), lambda b, h, i: (b, h, 0, 0))],
        out_specs=pl.BlockSpec((None, TM, C_HEADS_PER_STEP * C_V), lambda b, h, i: (b, i, h)),
        out_shape=jax.ShapeDtypeStruct((BATCH, TOK, C_HEADS * C_V), BF16),
        scratch_shapes=[pltpu.VMEM((C_HEADS_PER_STEP, 1, TM), F32), pltpu.VMEM((C_HEADS_PER_STEP, rows, TM), F32)],
        compiler_params=_params(3),
        name="mla_attn",
    )(q, k, vt)


def _post_odd_kernel(o_ref, x_ref, gate_ref, w_ref, gp_ref, out_ref):
    y = _dot(o_ref[...], w_ref[...])
    out_ref[...] = x_ref[...] + gate_ref[...] * (_rms(y) * gp_ref[...])


def _post_odd(layer, o, xs, mods, w_out, g_post, stream):
    width = C_HEADS * C_V
    in_spec = _tok_spec if stream.has_ctx else _latent_of_unified_spec
    return pl.pallas_call(
        _post_odd_kernel,
        grid=(stream.tiles,),
        in_specs=[in_spec(width), in_spec(D_MODEL), _mod_spec(layer, 2, stream), _full((width, D_MODEL)),
                  _full((1, D_MODEL))],
        out_specs=_tok_spec(D_MODEL),
        out_shape=jax.ShapeDtypeStruct((stream.rows, D_MODEL), F32),
        compiler_params=_params(1),
        name="post_odd",
    )(o, xs, mods, w_out, g_post)


def _ffn_kernel(x_ref, xp_ref, xn_ref, sh_ref, sc_ref, gate_ref, g_ref, gp_ref, wu_ref, cw_ref, cb_ref, wd_ref,
                o_ref, g_scr, act_scr, *, stream):
    prev_ok, next_ok = _halo_valid(stream)
    x = x_ref[...]
    h = _pre_norm(x, g_ref, sc_ref, sh_ref)
    h_prev = _pre_norm(xp_ref[...], g_ref, sc_ref, sh_ref)
    h_next = _pre_norm(xn_ref[...], g_ref, sc_ref, sh_ref)
    hb = h.astype(BF16)
    hb_ext = jnp.concatenate([h_prev, h, h_next], axis=0).astype(BF16)
    lo, hi = F_HALO, F_HALO + TM
    for c in range(FFN_HIDDEN // FFN_CHUNK):
        cols = slice(c * FFN_CHUNK, (c + 1) * FFN_CHUNK)
        gate = _dot(hb_ext, wu_ref[:, FFN_HIDDEN + c * FFN_CHUNK:FFN_HIDDEN + (c + 1) * FFN_CHUNK])
        g_scr[0:lo, :] = gate[0:lo] * prev_ok
        g_scr[lo:hi, :] = gate[lo:hi]
        g_scr[hi:, :] = gate[hi:] * next_ok
        conv = (cw_ref[0:1, cols] * g_scr[lo - 1:hi - 1, :] + cw_ref[1:2, cols] * gate[lo:hi]
                + cw_ref[2:3, cols] * g_scr[lo + 1:hi + 1, :] + cb_ref[:, cols])
        val = _dot(hb, wu_ref[:, cols])
        act_scr[:, cols] = (_silu(conv) * val).astype(BF16)
    y = _dot(act_scr[...], wd_ref[...])
    o_ref[...] = x + gate_ref[...] * (_rms(y) * gp_ref[...])


def _ffn(layer, xs, mods, g_pre, g_post, w_up, conv_w, conv_b, w_down, stream):
    return pl.pallas_call(
        functools.partial(_ffn_kernel, stream=stream),
        grid=(stream.tiles,),
        in_specs=[_tok_spec(D_MODEL), *_halo_specs(D_MODEL, F_HALO, stream),
                  _mod_spec(layer, 3, stream), _mod_spec(layer, 4, stream), _mod_spec(layer, 5, stream),
                  _full((1, D_MODEL)), _full((1, D_MODEL)), _full((D_MODEL, 2 * FFN_HIDDEN)),
                  _full((SUBLANES, FFN_HIDDEN)), _full((1, FFN_HIDDEN)), _full((FFN_HIDDEN, D_MODEL))],
        out_specs=_tok_spec(D_MODEL),
        out_shape=jax.ShapeDtypeStruct((stream.rows, D_MODEL), F32),
        scratch_shapes=[pltpu.VMEM((TM + 2 * F_HALO, FFN_CHUNK), F32), pltpu.VMEM((TM, FFN_HIDDEN), BF16)],
        compiler_params=_params(1),
        name="conv_ffn",
    )(xs, xs, xs, mods, mods, mods, g_pre, g_post, w_up, conv_w, conv_b, w_down)


def _pad_rows(w, rows):
    return jnp.concatenate([w, jnp.zeros((rows - w.shape[0],) + w.shape[1:], w.dtype)], axis=0)


def _pad_last(w, width):
    return jnp.concatenate([w, jnp.zeros(w.shape[:-1] + (width - w.shape[-1],), w.dtype)], axis=-1)


def _mla_weights(w_in, w_uq, w_ukv, w_out):
    rank = C_Q_RANK + C_KV_RANK
    pe = jnp.concatenate([jnp.zeros((D_MODEL, C_NOPE), F32), w_in[:, rank:],
                          jnp.zeros((D_MODEL, HEAD_PAD - C_NOPE - C_ROPE), F32)], axis=1)
    w_in_p = jnp.concatenate([w_in[:, :rank], pe], axis=1)
    w_q = _pad_last(w_uq.reshape(C_Q_RANK, C_HEADS, C_NOPE + C_ROPE), HEAD_PAD).reshape(C_Q_RANK, -1)
    kv = w_ukv.reshape(C_KV_RANK, C_HEADS, C_NOPE + C_V)
    w_k = _pad_last(kv[..., :C_NOPE], HEAD_PAD).reshape(C_KV_RANK, -1)
    w_vt = kv[..., C_NOPE:].reshape(C_KV_RANK, -1).T
    return tuple(w.astype(BF16) for w in (w_in_p, w_q, w_k, w_vt, w_out))


def kernel(x, c, ctx, c_ctx, ada_w, ada_b, norm_mix_pre, norm_mix_post, norm_ffn_pre, norm_ffn_post, ffn_w_up,
           ffn_conv_w, ffn_conv_b, ffn_w_down, ab_w_in, a_conv_w, a_conv_b, a_ln_g, a_ln_b, b_lambda, b_subln,
           ab_w_out, c_w_in, c_q_norm, c_kv_norm, c_w_uq, c_w_ukv, c_w_out):
    mods = _modulation(c, c_ctx, ada_w, ada_b)
    xs = jnp.concatenate([ctx, x], axis=1).reshape(NT, D_MODEL)
    rope_b = _rope_tables(B_HEAD_DIM, 0, B_HEAD_DIM)
    rope_c = _rope_tables(C_ROPE, C_NOPE, HEAD_PAD)
    row = lambda v: v.reshape(1, -1)
    tokens = lambda t: t.reshape(BATCH, TOK, t.shape[-1])
    flat = lambda t: t.reshape(NT, t.shape[-1])

    for i in range(DEPTH):
        j = i // 2
        stream = UNIFIED if i < DEPTH - 1 else LATENT
        if i % 2 == 0:
            lambda_init = 0.8 - 0.6 * math.exp(-0.3 * i)
            a, q, k, vt = _p1_even(i, xs, mods, row(norm_mix_pre[i]), ab_w_in[j], rope_b)
            ob = _diff_attn(tokens(q), tokens(k), vt, b_lambda[j], row(b_subln[j]), lambda_init)
            xs = _post_even(i, a, flat(ob), xs, mods, ab_w_out[j].astype(BF16),
                            _pad_rows(a_conv_w[j], A_KERNEL + 1), row(a_conv_b[j]), row(a_ln_g[j]),
                            row(a_ln_b[j]), row(norm_mix_post[i]))
        else:
            w_in, w_q, w_k, w_vt, w_o = _mla_weights(c_w_in[j], c_w_uq[j], c_w_ukv[j], c_w_out[j])
            q, k, vt = _p1_odd(i, xs, mods, row(norm_mix_pre[i]), w_in, row(c_q_norm[j]), row(c_kv_norm[j]),
                               w_q, w_k, w_vt, rope_c)
            o = _mla_attn(tokens(q), tokens(k), vt)
            xs = _post_odd(i, flat(o), xs, mods, w_o, row(norm_mix_post[i]), stream)
        xs = _ffn(i, xs, mods, row(norm_ffn_pre[i]), row(norm_ffn_post[i]), ffn_w_up[i].astype(BF16),
                  _pad_rows(ffn_conv_w[i], SUBLANES), row(ffn_conv_b[i]), ffn_w_down[i].astype(BF16), stream)
    return xs.reshape(BATCH, SEQ, D_MODEL)
```

```python
import functools
import math
from typing import NamedTuple

import jax
import jax.numpy as jnp
from jax import lax
from jax.experimental import pallas as pl
from jax.experimental.pallas import tpu as pltpu

D_MODEL = 1024
BATCH = 8
SEQ = 4096
DEPTH = 4
CTX_LEN = 256
GRID_W = 64
ROPE_BASE = 10000.0
RMS_EPS = 1e-6
LN_EPS = 1e-5

A_WIDTH = 512
A_KERNEL = 31
B_HEADS = 4
B_HEAD_DIM = 64
B_V_DIM = 2 * B_HEAD_DIM
B_QK_COLS = B_HEADS * 2 * B_HEAD_DIM
B_WIDTH = B_HEADS * B_V_DIM
AB_IN = 2 * A_WIDTH + 2 * B_QK_COLS + B_WIDTH

C_HEADS = 16
C_NOPE = 64
C_ROPE = 32
C_V = 64
C_Q_RANK = 768
C_KV_RANK = 256

FFN_HIDDEN = 2816
FFN_KERNEL = 3

LANES = 128
SUBLANES = 8
BF16_SUBLANES = 16
MXU_COLS = 256
VMEM_LIMIT_BYTES = 56 * 1024 * 1024

TOK = CTX_LEN + SEQ
NT = BATCH * TOK
TM = 256
TPB = TOK // TM
NTILES = NT // TM
MOD_ROWS = 16
CTX_ROW = BATCH
HEAD_PAD = LANES
C_HEADS_PER_STEP = 4
B_HEADS_PER_STEP = 4
KV_CHUNK = 512
SCORE_LOOKAHEAD = 4
FFN_CHUNK = MXU_COLS
FFN_TILES = 2
A_HALO = 16
F_HALO = SUBLANES

F32 = jnp.float32
BF16 = jnp.bfloat16
LOG2_E = math.log2(math.e)


def _dot(a, b):
    return jnp.dot(a, b, preferred_element_type=F32)


def _dot_nt(a, b):
    return lax.dot_general(a, b, (((1,), (1,)), ((), ())), preferred_element_type=F32)


def _rms(x, eps=RMS_EPS):
    return x * lax.rsqrt(jnp.mean(x * x, axis=-1, keepdims=True) + eps)


def _silu(x):
    return x * jax.nn.sigmoid(x)


def _params(n_axes):
    return pltpu.CompilerParams(dimension_semantics=("parallel",) * n_axes, vmem_limit_bytes=VMEM_LIMIT_BYTES)


def _full(shape):
    return pl.BlockSpec(shape, lambda *_: (0,) * len(shape), pipeline_mode=pl.Buffered(1))


def _tile_in_batch(t):
    return t % TPB


class _Stream(NamedTuple):
    tiles_per_batch: int
    has_ctx: bool

    @property
    def tiles(self):
        return BATCH * self.tiles_per_batch

    @property
    def rows(self):
        return self.tiles * TM


UNIFIED = _Stream(TPB, True)
LATENT = _Stream(TPB - 1, False)
assert DEPTH % 2 == 0, "only the odd-layer tail has a latent-only variant"


def _mod_spec(layer, slot, stream=UNIFIED, sub=0, per_step=1):
    def index(t):
        tile = t * per_step + sub
        row = tile // stream.tiles_per_batch
        if stream.has_ctx:
            row = jnp.where(tile % stream.tiles_per_batch == 0, CTX_ROW, row)
        return ((layer * MOD_ROWS + row) * 6 + slot, 0, 0)
    return pl.BlockSpec((None, 1, D_MODEL), index)


def _tok_spec(width, per_step=1):
    return pl.BlockSpec((per_step * TM, width), lambda t: (t, 0))


def _latent_of_unified_spec(width):
    return pl.BlockSpec((TM, width), lambda t: ((t // LATENT.tiles_per_batch) * TPB + t % LATENT.tiles_per_batch + 1, 0))


def _halo_specs(width, rows, stream=UNIFIED, per_step=1):
    blocks = per_step * TM // rows
    prev = lambda t: (jnp.maximum(t * blocks - 1, 0), 0)
    nxt = lambda t: (jnp.minimum((t + 1) * blocks, stream.rows // rows - 1), 0)
    return [pl.BlockSpec((rows, width), prev), pl.BlockSpec((rows, width), nxt)]


def _halo_valid(stream, tile=None):
    r = (pl.program_id(0) if tile is None else tile) % stream.tiles_per_batch
    first_latent = 1 if stream.has_ctx else 0
    return ((r > first_latent).astype(F32),
            ((r >= first_latent) & (r < stream.tiles_per_batch - 1)).astype(F32))


def _vt_spec(heads, rows):
    return pl.BlockSpec((None, heads, rows, TM), lambda t: (t // TPB, 0, 0, _tile_in_batch(t)))


def _store_vt(vt_ref, vt, heads, dim):
    for h in range(heads):
        vt_ref[h, 0:dim, :] = vt[h * dim:(h + 1) * dim, :].astype(BF16)
        vt_ref[h, dim:dim + BF16_SUBLANES, :] = jnp.ones((BF16_SUBLANES, TM), BF16)


def _rope_spec():
    return pl.BlockSpec((TM, LANES), lambda t: (_tile_in_batch(t), 0))


def _rope(y, c, s_up, s_dn, shift):
    return y * c + pltpu.roll(y, LANES - shift, 1) * s_up + pltpu.roll(y, shift, 1) * s_dn


def _rope_tables(group, lane_offset, period):
    half = group // 4
    lane = jnp.arange(LANES)
    rel = (lane % period) - lane_offset
    active = (rel >= 0) & (rel < group)
    rel = jnp.where(active, rel, 0)
    by_col = rel >= group // 2
    within = rel % (group // 2)
    freq_idx = within % half
    inv_freq = ROPE_BASE ** (-freq_idx.astype(F32) / half)
    t = jnp.arange(SEQ, dtype=jnp.int32)
    pos = jnp.where(by_col[None, :], (t % GRID_W)[:, None], (t // GRID_W)[:, None]).astype(F32)
    ang = pos * inv_freq[None, :]
    cos, sin = jnp.cos(ang), jnp.sin(ang)
    first = within < half
    c = jnp.where(active[None, :], cos, 1.0)
    s_up = jnp.where((active & first)[None, :], -sin, 0.0)
    s_dn = jnp.where((active & ~first)[None, :], sin, 0.0)
    ident = [jnp.ones((CTX_LEN, LANES), F32), jnp.zeros((CTX_LEN, LANES), F32), jnp.zeros((CTX_LEN, LANES), F32)]
    return tuple(jnp.concatenate([i, x.astype(F32)], axis=0) for i, x in zip(ident, (c, s_up, s_dn)))


def _mods_kernel(c_ref, w_ref, b_ref, o_ref):
    s = _silu(c_ref[...])
    o_ref[...] = jnp.dot(s, w_ref[...], preferred_element_type=F32, precision=lax.Precision.HIGHEST) + b_ref[...]


def _modulation(c, c_ctx, ada_w, ada_b):
    rows = jnp.concatenate([c, c_ctx[None, :], jnp.zeros((MOD_ROWS - BATCH - 1, D_MODEL), F32)], axis=0)
    n_blk = 1536
    out = pl.pallas_call(
        _mods_kernel,
        grid=(DEPTH, 6 * D_MODEL // n_blk),
        in_specs=[
            pl.BlockSpec((MOD_ROWS, D_MODEL), lambda i, n: (0, 0)),
            pl.BlockSpec((None, D_MODEL, n_blk), lambda i, n: (i, 0, n)),
            pl.BlockSpec((None, 1, n_blk), lambda i, n: (i, 0, n)),
        ],
        out_specs=pl.BlockSpec((None, MOD_ROWS, n_blk), lambda i, n: (i, 0, n)),
        out_shape=jax.ShapeDtypeStruct((DEPTH, MOD_ROWS, 6 * D_MODEL), F32),
        compiler_params=_params(2),
        name="modulation",
    )(rows, ada_w, ada_b.reshape(DEPTH, 1, 6 * D_MODEL))
    return out.reshape(DEPTH * MOD_ROWS * 6, 1, D_MODEL)


def _pre_norm(x, g_ref, sc_ref, sh_ref):
    return _rms(x) * (g_ref[...] * (1.0 + sc_ref[...])) + sh_ref[...]


def _p1_even_kernel(x_ref, sh_ref, sc_ref, g_ref, w_ref, wvt_ref, rc_ref, ru_ref, rd_ref, a_ref, q_ref, k_ref, vt_ref):
    hb = _pre_norm(x_ref[...], g_ref, sc_ref, sh_ref).astype(BF16)
    a_ref[...] = _dot(hb, w_ref[:, 0:2 * A_WIDTH]).astype(BF16)
    c, s_up, s_dn = rc_ref[...], ru_ref[...], rd_ref[...]
    o = 2 * A_WIDTH
    for col, out_ref, scale in ((o, q_ref, B_HEAD_DIM ** -0.5 * LOG2_E), (o + B_QK_COLS, k_ref, 1.0)):
        y = _dot(hb, w_ref[:, col:col + B_QK_COLS])
        for j in range(B_QK_COLS // LANES):
            yj = _rope(y[:, j * LANES:(j + 1) * LANES], c, s_up, s_dn, B_HEAD_DIM // 4)
            out_ref[:, j * LANES:(j + 1) * LANES] = (yj * scale).astype(BF16)
    _store_vt(vt_ref, _dot_nt(wvt_ref[...], hb), B_HEADS, B_V_DIM)


def _p1_even(layer, xs, mods, g, w_in, rope):
    qk_cols = 2 * A_WIDTH + 2 * B_QK_COLS
    vt_rows = B_V_DIM + BF16_SUBLANES
    return pl.pallas_call(
        _p1_even_kernel,
        grid=(NTILES,),
        in_specs=[_tok_spec(D_MODEL), _mod_spec(layer, 0), _mod_spec(layer, 1), _full((1, D_MODEL)),
                  _full((D_MODEL, qk_cols)), _full((B_WIDTH, D_MODEL)), _rope_spec(), _rope_spec(), _rope_spec()],
        out_specs=[_tok_spec(2 * A_WIDTH), _tok_spec(B_QK_COLS), _tok_spec(B_QK_COLS), _vt_spec(B_HEADS, vt_rows)],
        out_shape=[jax.ShapeDtypeStruct((NT, 2 * A_WIDTH), BF16), jax.ShapeDtypeStruct((NT, B_QK_COLS), BF16),
                   jax.ShapeDtypeStruct((NT, B_QK_COLS), BF16),
                   jax.ShapeDtypeStruct((BATCH, B_HEADS, vt_rows, TOK), BF16)],
        compiler_params=_params(1),
        name="p1_even",
    )(xs, mods, mods, g, w_in[:, :qk_cols].astype(BF16), w_in[:, qk_cols:].T.astype(BF16), *rope)


def _softmax_step(st, vt, state):
    m_chunk = jnp.max(st, axis=0, keepdims=True)
    m_new = m_chunk if state is None else jnp.maximum(state[0], m_chunk)
    pv = _dot(vt, jnp.exp2((st - m_new).astype(BF16)))
    if state is None:
        return m_new, pv
    return m_new, jnp.exp2(state[0] - m_new) * state[1] + pv


def _attend(streams, scores, values, finish):
    ctx_only = [(0, CTX_LEN)]
    everything = ctx_only + [(CTX_LEN + i * KV_CHUNK, KV_CHUNK) for i in range(SEQ // KV_CHUNK)]
    qi = pl.program_id(2)
    for chunks, cond in ((ctx_only, qi == 0), (everything, qi > 0)):
        items = [(start, size, s) for start, size in chunks for s in range(streams)]

        @pl.when(cond)
        def _(items=items):
            state = [None] * streams
            pending = [scores(*item) for item in items[:SCORE_LOOKAHEAD]]
            for n, item in enumerate(items):
                if n + SCORE_LOOKAHEAD < len(items):
                    pending.append(scores(*items[n + SCORE_LOOKAHEAD]))
                state[item[2]] = _softmax_step(pending.pop(0), values(*item), state[item[2]])
            finish([acc for _, acc in state])


def _diff_attn_kernel(lam_ref, sub_ref, q_ref, k_ref, vt_ref, o_ref, *, lambda_init):
    lane = lax.broadcasted_iota(jnp.int32, (TM, LANES), 1)
    zero = jnp.zeros((TM, LANES), BF16)
    qs = []
    for h in range(B_HEADS_PER_STEP):
        q = q_ref[:, h * LANES:(h + 1) * LANES]
        qs += [jnp.where(lane < B_HEAD_DIM, q, zero), jnp.where(lane >= B_HEAD_DIM, q, zero)]

    def scores(start, size, s):
        h = s // 2
        return _dot_nt(k_ref[start:start + size, h * LANES:(h + 1) * LANES], qs[s])

    def values(start, size, s):
        return vt_ref[s // 2, :, start:start + size]

    def finish(accs):
        lv = lam_ref[...]
        lam = (jnp.exp(jnp.sum(lv[0:1] * lv[1:2], axis=-1, keepdims=True))
               - jnp.exp(jnp.sum(lv[2:3] * lv[3:4], axis=-1, keepdims=True)) + lambda_init)
        for h in range(B_HEADS_PER_STEP):
            o1, o2 = (acc[0:B_V_DIM, :] / acc[B_V_DIM:B_V_DIM + 1, :] for acc in accs[2 * h:2 * h + 2])
            o = (o1 - lam * o2).T
            o_ref[:, h * B_V_DIM:(h + 1) * B_V_DIM] = (_rms(o) * (sub_ref[...] * (1.0 - lambda_init))).astype(BF16)

    _attend(2 * B_HEADS_PER_STEP, scores, values, finish)


def _diff_attn(q, k, vt, lam_vecs, subln_g, lambda_init):
    rows = B_V_DIM + BF16_SUBLANES
    width = B_HEADS_PER_STEP * LANES
    return pl.pallas_call(
        functools.partial(_diff_attn_kernel, lambda_init=lambda_init),
        grid=(BATCH, B_HEADS // B_HEADS_PER_STEP, TPB),
        in_specs=[pl.BlockSpec((4, B_HEAD_DIM), lambda b, h, i: (0, 0)),
                  pl.BlockSpec((1, B_V_DIM), lambda b, h, i: (0, 0)),
                  pl.BlockSpec((None, TM, width), lambda b, h, i: (b, i, h)),
                  pl.BlockSpec((None, TOK, width), lambda b, h, i: (b, 0, h)),
                  pl.BlockSpec((None, B_HEADS_PER_STEP, rows, TOK), lambda b, h, i: (b, h, 0, 0))],
        out_specs=pl.BlockSpec((None, TM, B_HEADS_PER_STEP * B_V_DIM), lambda b, h, i: (b, i, h)),
        out_shape=jax.ShapeDtypeStruct((BATCH, TOK, B_WIDTH), BF16),
        compiler_params=_params(3),
        name="diff_attn",
    )(lam_vecs, subln_g, q, k, vt)


def _post_even_kernel(a_ref, ap_ref, an_ref, ob_ref, x_ref, gate_ref, w_ref, cw_ref, cb_ref, lg_ref, lb_ref,
                      gp_ref, o_ref, u_ref):
    prev_ok, next_ok = _halo_valid(UNIFIED)

    def glu(a):
        a = a.astype(F32)
        return a[:, :A_WIDTH] * jax.nn.sigmoid(a[:, A_WIDTH:])

    ext = TM + 2 * A_HALO
    u_ref[0, 0:A_HALO, :] = glu(ap_ref[...]) * prev_ok
    u_ref[0, A_HALO:A_HALO + TM, :] = glu(a_ref[...])
    u_ref[0, A_HALO + TM:, :] = glu(an_ref[...]) * next_ok
    for s in range(1, SUBLANES):
        u_ref[s, 0:ext - SUBLANES, :] = u_ref[0, s:s + ext - SUBLANES, :]

    rows = 32
    base = A_HALO - A_KERNEL // 2
    outs = []
    for r0 in range(0, TM, rows):
        acc = jnp.broadcast_to(cb_ref[...], (rows, A_WIDTH))
        for tap in range(A_KERNEL):
            s = (base + tap) % SUBLANES
            i0 = r0 + base + tap - s
            acc = acc + cw_ref[tap:tap + 1, :] * u_ref[s, i0:i0 + rows, :]
        mu = jnp.mean(acc, axis=-1, keepdims=True)
        cen = acc - mu
        var = jnp.mean(cen * cen, axis=-1, keepdims=True)
        y = cen * lax.rsqrt(var + LN_EPS) * lg_ref[...] + lb_ref[...]
        outs.append(_silu(y).astype(BF16))
    ya = jnp.concatenate(outs, axis=0)
    y = _dot(ya, w_ref[0:A_WIDTH, :]) + _dot(ob_ref[...], w_ref[A_WIDTH:, :])
    o_ref[...] = x_ref[...] + gate_ref[...] * (_rms(y) * gp_ref[...])


def _post_even(layer, a, ob, xs, mods, w_out, conv_w, conv_b, ln_g, ln_b, g_post):
    return pl.pallas_call(
        _post_even_kernel,
        grid=(NTILES,),
        in_specs=[_tok_spec(2 * A_WIDTH), *_halo_specs(2 * A_WIDTH, A_HALO),
                  _tok_spec(B_WIDTH), _tok_spec(D_MODEL), _mod_spec(layer, 2),
                  _full((A_WIDTH + B_WIDTH, D_MODEL)), _full((A_KERNEL + 1, A_WIDTH)), _full((1, A_WIDTH)),
                  _full((1, A_WIDTH)), _full((1, A_WIDTH)), _full((1, D_MODEL))],
        out_specs=_tok_spec(D_MODEL),
        out_shape=jax.ShapeDtypeStruct((NT, D_MODEL), F32),
        scratch_shapes=[pltpu.VMEM((SUBLANES, TM + 2 * A_HALO, A_WIDTH), F32)],
        compiler_params=_params(1),
        name="post_even",
    )(a, a, a, ob, xs, mods, w_out, conv_w, conv_b, ln_g, ln_b, g_post)


def _p1_odd_kernel(x_ref, sh_ref, sc_ref, g_ref, w_ref, gq_ref, gkv_ref, wq_ref, wk_ref, wvt_ref,
                   rc_ref, ru_ref, rd_ref, q_ref, k_ref, vt_ref):
    hb = _pre_norm(x_ref[...], g_ref, sc_ref, sh_ref).astype(BF16)
    p = _dot(hb, w_ref[...])
    c, s_up, s_dn = rc_ref[...], ru_ref[...], rd_ref[...]
    shift = C_ROPE // 4
    scale = (C_NOPE + C_ROPE) ** -0.5 * LOG2_E

    cq =(_rms(p[:, :C_Q_RANK]) * gq_ref[...]).astype(BF16)
    q = _dot(cq, wq_ref[...])
    for j in range(C_HEADS):
        sl = slice(j * HEAD_PAD, (j + 1) * HEAD_PAD)
        q_ref[:, sl] = (_rope(q[:, sl], c, s_up, s_dn, shift) * scale).astype(BF16)

    ckv = (_rms(p[:, C_Q_RANK:C_Q_RANK + C_KV_RANK]) * gkv_ref[...]).astype(BF16)
    k_pe = _rope(p[:, C_Q_RANK + C_KV_RANK:], c, s_up, s_dn, shift)
    k = _dot(ckv, wk_ref[...])
    for j in range(C_HEADS):
        sl = slice(j * HEAD_PAD, (j + 1) * HEAD_PAD)
        k_ref[:, sl] = (k[:, sl] + k_pe).astype(BF16)
    _store_vt(vt_ref, _dot_nt(wvt_ref[...], ckv), C_HEADS, C_V)


def _p1_odd(layer, xs, mods, g, w_in, g_q, g_kv, w_q, w_k, w_vt, rope):
    width = C_HEADS * HEAD_PAD
    v_width = C_HEADS * C_V
    vt_rows = C_V + BF16_SUBLANES
    in_cols = C_Q_RANK + C_KV_RANK + HEAD_PAD
    out = jax.ShapeDtypeStruct((NT, width), BF16)
    return pl.pallas_call(
        _p1_odd_kernel,
        grid=(NTILES,),
        in_specs=[_tok_spec(D_MODEL), _mod_spec(layer, 0), _mod_spec(layer, 1), _full((1, D_MODEL)),
                  _full((D_MODEL, in_cols)), _full((1, C_Q_RANK)), _full((1, C_KV_RANK)),
                  _full((C_Q_RANK, width)), _full((C_KV_RANK, width)), _full((v_width, C_KV_RANK)),
                  _rope_spec(), _rope_spec(), _rope_spec()],
        out_specs=[_tok_spec(width), _tok_spec(width), _vt_spec(C_HEADS, vt_rows)],
        out_shape=[out, out, jax.ShapeDtypeStruct((BATCH, C_HEADS, vt_rows, TOK), BF16)],
        compiler_params=_params(1),
        name="p1_odd",
    )(xs, mods, mods, g, w_in, g_q, g_kv, w_q, w_k, w_vt, *rope)


def _mla_kernel(q_ref, k_ref, vt_ref, o_ref):
    def scores(start, size, h):
        sl = slice(h * HEAD_PAD, (h + 1) * HEAD_PAD)
        return _dot_nt(k_ref[start:start + size, sl], q_ref[:, sl])

    def values(start, size, h):
        return vt_ref[h, :, start:start + size]

    def finish(accs):
        o_t = [acc[0:C_V, :] / acc[C_V:C_V + 1, :] for acc in accs]
        o_ref[...] = jnp.concatenate(o_t, axis=0).T.astype(BF16)

    _attend(C_HEADS_PER_STEP, scores, values, finish)


def _mla_attn(q, k, vt):
    width = C_HEADS_PER_STEP * HEAD_PAD
    rows = C_V + BF16_SUBLANES
    return pl.pallas_call(
        _mla_kernel,
        grid=(BATCH, C_HEADS // C_HEADS_PER_STEP, TPB),
        in_specs=[pl.BlockSpec((None, TM, width), lambda b, h, i: (b, i, h)),
                  pl.BlockSpec((None, TOK, width), lambda b, h, i: (b, 0, h)),
                  pl.BlockSpec((None, C_HEADS_PER_STEP, rows, TOK), lambda b, h, i: (b, h, 0, 0))],
        out_specs=pl.BlockSpec((None, TM, C_HEADS_PER_STEP * C_V), lambda b, h, i: (b, i, h)),
        out_shape=jax.ShapeDtypeStruct((BATCH, TOK, C_HEADS * C_V), BF16),
        compiler_params=_params(3),
        name="mla_attn",
    )(q, k, vt)


def _post_odd_kernel(o_ref, x_ref, gate_ref, w_ref, gp_ref, out_ref):
    y = _dot(o_ref[...], w_ref[...])
    out_ref[...] = x_ref[...] + gate_ref[...] * (_rms(y) * gp_ref[...])


def _post_odd(layer, o, xs, mods, w_out, g_post, stream):
    width = C_HEADS * C_V
    in_spec = _tok_spec if stream.has_ctx else _latent_of_unified_spec
    return pl.pallas_call(
        _post_odd_kernel,
        grid=(stream.tiles,),
        in_specs=[in_spec(width), in_spec(D_MODEL), _mod_spec(layer, 2, stream), _full((width, D_MODEL)),
                  _full((1, D_MODEL))],
        out_specs=_tok_spec(D_MODEL),
        out_shape=jax.ShapeDtypeStruct((stream.rows, D_MODEL), F32),
        compiler_params=_params(1),
        name="post_odd",
    )(o, xs, mods, w_out, g_post)


def _ffn_tile(x, x_prev, x_next, prev_ok, next_ok, sh_ref, sc_ref, gate_ref, g_ref, gp_ref, wu_ref, cw_ref, cb_ref,
              wd_ref, g_scr, act_scr):
    h = _pre_norm(x, g_ref, sc_ref, sh_ref)
    h_prev = _pre_norm(x_prev, g_ref, sc_ref, sh_ref)
    h_next = _pre_norm(x_next, g_ref, sc_ref, sh_ref)
    hb = h.astype(BF16)
    hb_ext = jnp.concatenate([h_prev, h, h_next], axis=0).astype(BF16)
    lo, hi = F_HALO, F_HALO + TM
    for c in range(FFN_HIDDEN // FFN_CHUNK):
        cols = slice(c * FFN_CHUNK, (c + 1) * FFN_CHUNK)
        gate = _dot(hb_ext, wu_ref[:, FFN_HIDDEN + c * FFN_CHUNK:FFN_HIDDEN + (c + 1) * FFN_CHUNK])
        g_scr[0:lo, :] = gate[0:lo] * prev_ok
        g_scr[lo:hi, :] = gate[lo:hi]
        g_scr[hi:, :] = gate[hi:] * next_ok
        conv = (cw_ref[0:1, cols] * g_scr[lo - 1:hi - 1, :] + cw_ref[1:2, cols] * gate[lo:hi]
                + cw_ref[2:3, cols] * g_scr[lo + 1:hi + 1, :] + cb_ref[:, cols])
        val = _dot(hb, wu_ref[:, cols])
        act_scr[:, cols] = (_silu(conv) * val).astype(BF16)
    y = _dot(act_scr[...], wd_ref[...])
    return x + gate_ref[...] * (_rms(y) * gp_ref[...])


def _ffn_kernel(x_ref, xp_ref, xn_ref, *refs, stream):
    mods = refs[:3 * FFN_TILES]
    g_ref, gp_ref, wu_ref, cw_ref, cb_ref, wd_ref, o_ref, g_scr, act_scr = refs[3 * FFN_TILES:]
    for s in range(FFN_TILES):
        rows = slice(s * TM, (s + 1) * TM)
        x_prev = xp_ref[...] if s == 0 else x_ref[s * TM - F_HALO:s * TM, :]
        x_next = xn_ref[...] if s == FFN_TILES - 1 else x_ref[(s + 1) * TM:(s + 1) * TM + F_HALO, :]
        prev_ok, next_ok = _halo_valid(stream, pl.program_id(0) * FFN_TILES + s)
        o_ref[rows, :] = _ffn_tile(x_ref[rows, :], x_prev, x_next, prev_ok, next_ok, *mods[3 * s:3 * s + 3],
                                   g_ref, gp_ref, wu_ref, cw_ref, cb_ref, wd_ref, g_scr.at[s], act_scr.at[s])


def _ffn(layer, xs, mods, g_pre, g_post, w_up, conv_w, conv_b, w_down, stream):
    mod_specs = [_mod_spec(layer, slot, stream, s, FFN_TILES) for s in range(FFN_TILES) for slot in (3, 4, 5)]
    return pl.pallas_call(
        functools.partial(_ffn_kernel, stream=stream),
        grid=(stream.tiles // FFN_TILES,),
        in_specs=[_tok_spec(D_MODEL, FFN_TILES), *_halo_specs(D_MODEL, F_HALO, stream, FFN_TILES), *mod_specs,
                  _full((1, D_MODEL)), _full((1, D_MODEL)), _full((D_MODEL, 2 * FFN_HIDDEN)),
                  _full((SUBLANES, FFN_HIDDEN)), _full((1, FFN_HIDDEN)), _full((FFN_HIDDEN, D_MODEL))],
        out_specs=_tok_spec(D_MODEL, FFN_TILES),
        out_shape=jax.ShapeDtypeStruct((stream.rows, D_MODEL), F32),
        scratch_shapes=[pltpu.VMEM((FFN_TILES, TM + 2 * F_HALO, FFN_CHUNK), F32),
                        pltpu.VMEM((FFN_TILES, TM, FFN_HIDDEN), BF16)],
        compiler_params=_params(1),
        name="conv_ffn",
    )(xs, xs, xs, *([mods] * len(mod_specs)), g_pre, g_post, w_up, conv_w, conv_b, w_down)


def _pad_rows(w, rows):
    return jnp.concatenate([w, jnp.zeros((rows - w.shape[0],) + w.shape[1:], w.dtype)], axis=0)


def _pad_last(w, width):
    return jnp.concatenate([w, jnp.zeros(w.shape[:-1] + (width - w.shape[-1],), w.dtype)], axis=-1)


def _mla_weights(w_in, w_uq, w_ukv, w_out):
    rank = C_Q_RANK + C_KV_RANK
    pe = jnp.concatenate([jnp.zeros((D_MODEL, C_NOPE), F32), w_in[:, rank:],
                          jnp.zeros((D_MODEL, HEAD_PAD - C_NOPE - C_ROPE), F32)], axis=1)
    w_in_p = jnp.concatenate([w_in[:, :rank], pe], axis=1)
    w_q = _pad_last(w_uq.reshape(C_Q_RANK, C_HEADS, C_NOPE + C_ROPE), HEAD_PAD).reshape(C_Q_RANK, -1)
    kv = w_ukv.reshape(C_KV_RANK, C_HEADS, C_NOPE + C_V)
    w_k = _pad_last(kv[..., :C_NOPE], HEAD_PAD).reshape(C_KV_RANK, -1)
    w_vt = kv[..., C_NOPE:].reshape(C_KV_RANK, -1).T
    return tuple(w.astype(BF16) for w in (w_in_p, w_q, w_k, w_vt, w_out))


def kernel(x, c, ctx, c_ctx, ada_w, ada_b, norm_mix_pre, norm_mix_post, norm_ffn_pre, norm_ffn_post, ffn_w_up,
           ffn_conv_w, ffn_conv_b, ffn_w_down, ab_w_in, a_conv_w, a_conv_b, a_ln_g, a_ln_b, b_lambda, b_subln,
           ab_w_out, c_w_in, c_q_norm, c_kv_norm, c_w_uq, c_w_ukv, c_w_out):
    mods = _modulation(c, c_ctx, ada_w, ada_b)
    xs = jnp.concatenate([ctx, x], axis=1).reshape(NT, D_MODEL)
    rope_b = _rope_tables(B_HEAD_DIM, 0, B_HEAD_DIM)
    rope_c = _rope_tables(C_ROPE, C_NOPE, HEAD_PAD)
    row = lambda v: v.reshape(1, -1)
    tokens = lambda t: t.reshape(BATCH, TOK, t.shape[-1])
    flat = lambda t: t.reshape(NT, t.shape[-1])

    for i in range(DEPTH):
        j = i // 2
        stream = UNIFIED if i < DEPTH - 1 else LATENT
        if i % 2 == 0:
            lambda_init = 0.8 - 0.6 * math.exp(-0.3 * i)
            a, q, k, vt = _p1_even(i, xs, mods, row(norm_mix_pre[i]), ab_w_in[j], rope_b)
            ob = _diff_attn(tokens(q), tokens(k), vt, b_lambda[j], row(b_subln[j]), lambda_init)
            xs = _post_even(i, a, flat(ob), xs, mods, ab_w_out[j].astype(BF16),
                            _pad_rows(a_conv_w[j], A_KERNEL + 1), row(a_conv_b[j]), row(a_ln_g[j]),
                            row(a_ln_b[j]), row(norm_mix_post[i]))
        else:
            w_in, w_q, w_k, w_vt, w_o = _mla_weights(c_w_in[j], c_w_uq[j], c_w_ukv[j], c_w_out[j])
            q, k, vt = _p1_odd(i, xs, mods, row(norm_mix_pre[i]), w_in, row(c_q_norm[j]), row(c_kv_norm[j]),
                               w_q, w_k, w_vt, rope_c)
            o = _mla_attn(tokens(q), tokens(k), vt)
            xs = _post_odd(i, flat(o), xs, mods, w_o, row(norm_mix_post[i]), stream)
        xs = _ffn(i, xs, mods, row(norm_ffn_pre[i]), row(norm_ffn_post[i]), ffn_w_up[i].astype(BF16),
                  _pad_rows(ffn_conv_w[i], SUBLANES), row(ffn_conv_b[i]), ffn_w_down[i].astype(BF16), stream)
    return xs.reshape(BATCH, SEQ, D_MODEL)
```

```python
import functools
import math
from typing import NamedTuple

import jax
import jax.numpy as jnp
from jax import lax
from jax.experimental import pallas as pl
from jax.experimental.pallas import tpu as pltpu

D_MODEL = 1024
BATCH = 8
SEQ = 4096
DEPTH = 4
CTX_LEN = 256
GRID_W = 64
ROPE_BASE = 10000.0
RMS_EPS = 1e-6
LN_EPS = 1e-5

A_WIDTH = 512
A_KERNEL = 31
B_HEADS = 4
B_HEAD_DIM = 64
B_V_DIM = 2 * B_HEAD_DIM
B_QK_COLS = B_HEADS * 2 * B_HEAD_DIM
B_WIDTH = B_HEADS * B_V_DIM
AB_IN = 2 * A_WIDTH + 2 * B_QK_COLS + B_WIDTH

C_HEADS = 16
C_NOPE = 64
C_ROPE = 32
C_V = 64
C_Q_RANK = 768
C_KV_RANK = 256

FFN_HIDDEN = 2816
FFN_KERNEL = 3

LANES = 128
SUBLANES = 8
BF16_SUBLANES = 16
MXU_COLS = 256
VMEM_LIMIT_BYTES = 56 * 1024 * 1024

TOK = CTX_LEN + SEQ
NT = BATCH * TOK
TM = 256
TPB = TOK // TM
NTILES = NT // TM
MOD_ROWS = 16
CTX_ROW = BATCH
HEAD_PAD = LANES
C_HEADS_PER_STEP = 4
B_HEADS_PER_STEP = 4
SCORE_CHUNK = 1024
KV_CHUNK = 256
B_SCORE_LOOKAHEAD = 3
C_SCORE_LOOKAHEAD = 2
FFN_CHUNK = MXU_COLS
FFN_TILES = 1
A_HALO = 16
F_HALO = SUBLANES

F32 = jnp.float32
BF16 = jnp.bfloat16
LOG2_E = math.log2(math.e)


def _dot(a, b):
    return jnp.dot(a, b, preferred_element_type=F32)


def _dot_nt(a, b):
    return lax.dot_general(a, b, (((1,), (1,)), ((), ())), preferred_element_type=F32)


def _rms(x, eps=RMS_EPS):
    return x * lax.rsqrt(jnp.mean(x * x, axis=-1, keepdims=True) + eps)


def _silu(x):
    return x * jax.nn.sigmoid(x)


def _params(n_axes):
    return pltpu.CompilerParams(dimension_semantics=("parallel",) * n_axes, vmem_limit_bytes=VMEM_LIMIT_BYTES)


def _full(shape):
    return pl.BlockSpec(shape, lambda *_: (0,) * len(shape), pipeline_mode=pl.Buffered(1))


def _tile_in_batch(t):
    return t % TPB


class _Stream(NamedTuple):
    tiles_per_batch: int
    has_ctx: bool

    @property
    def tiles(self):
        return BATCH * self.tiles_per_batch

    @property
    def rows(self):
        return self.tiles * TM


UNIFIED = _Stream(TPB, True)
LATENT = _Stream(TPB - 1, False)
assert DEPTH % 2 == 0, "only the odd-layer tail has a latent-only variant"


def _mod_spec(layer, slot, stream=UNIFIED, sub=0, per_step=1):
    def index(t):
        tile = t * per_step + sub
        row = tile // stream.tiles_per_batch
        if stream.has_ctx:
            row = jnp.where(tile % stream.tiles_per_batch == 0, CTX_ROW, row)
        return ((layer * MOD_ROWS + row) * 6 + slot, 0, 0)
    return pl.BlockSpec((None, 1, D_MODEL), index)


def _tok_spec(width, per_step=1):
    return pl.BlockSpec((per_step * TM, width), lambda t: (t, 0))


def _latent_of_unified_spec(width):
    return pl.BlockSpec((TM, width), lambda t: ((t // LATENT.tiles_per_batch) * TPB + t % LATENT.tiles_per_batch + 1, 0))


def _halo_specs(width, rows, stream=UNIFIED, per_step=1):
    blocks = per_step * TM // rows
    prev = lambda t: (jnp.maximum(t * blocks - 1, 0), 0)
    nxt = lambda t: (jnp.minimum((t + 1) * blocks, stream.rows // rows - 1), 0)
    return [pl.BlockSpec((rows, width), prev), pl.BlockSpec((rows, width), nxt)]


def _halo_valid(stream, tile=None):
    r = (pl.program_id(0) if tile is None else tile) % stream.tiles_per_batch
    first_latent = 1 if stream.has_ctx else 0
    return ((r > first_latent).astype(F32),
            ((r >= first_latent) & (r < stream.tiles_per_batch - 1)).astype(F32))


def _vt_spec(heads, rows):
    return pl.BlockSpec((None, heads, rows, TM), lambda t: (t // TPB, 0, 0, _tile_in_batch(t)))


def _store_vt(vt_ref, vt, heads, dim):
    for h in range(heads):
        vt_ref[h, 0:dim, :] = vt[h * dim:(h + 1) * dim, :].astype(BF16)
        vt_ref[h, dim:dim + BF16_SUBLANES, :] = jnp.ones((BF16_SUBLANES, TM), BF16)


def _rope_spec():
    return pl.BlockSpec((TM, LANES), lambda t: (_tile_in_batch(t), 0))


def _rope(y, c, s_up, s_dn, shift):
    return y * c + pltpu.roll(y, LANES - shift, 1) * s_up + pltpu.roll(y, shift, 1) * s_dn


def _rope_tables(group, lane_offset, period):
    half = group // 4
    lane = jnp.arange(LANES)
    rel = (lane % period) - lane_offset
    active = (rel >= 0) & (rel < group)
    rel = jnp.where(active, rel, 0)
    by_col = rel >= group // 2
    within = rel % (group // 2)
    freq_idx = within % half
    inv_freq = ROPE_BASE ** (-freq_idx.astype(F32) / half)
    t = jnp.arange(SEQ, dtype=jnp.int32)
    pos = jnp.where(by_col[None, :], (t % GRID_W)[:, None], (t // GRID_W)[:, None]).astype(F32)
    ang = pos * inv_freq[None, :]
    cos, sin = jnp.cos(ang), jnp.sin(ang)
    first = within < half
    c = jnp.where(active[None, :], cos, 1.0)
    s_up = jnp.where((active & first)[None, :], -sin, 0.0)
    s_dn = jnp.where((active & ~first)[None, :], sin, 0.0)
    ident = [jnp.ones((CTX_LEN, LANES), F32), jnp.zeros((CTX_LEN, LANES), F32), jnp.zeros((CTX_LEN, LANES), F32)]
    return tuple(jnp.concatenate([i, x.astype(F32)], axis=0) for i, x in zip(ident, (c, s_up, s_dn)))


def _mods_kernel(c_ref, w_ref, b_ref, o_ref):
    s = _silu(c_ref[...])
    o_ref[...] = jnp.dot(s, w_ref[...], preferred_element_type=F32, precision=lax.Precision.HIGHEST) + b_ref[...]


def _modulation(c, c_ctx, ada_w, ada_b):
    rows = jnp.concatenate([c, c_ctx[None, :], jnp.zeros((MOD_ROWS - BATCH - 1, D_MODEL), F32)], axis=0)
    n_blk = 1536
    out = pl.pallas_call(
        _mods_kernel,
        grid=(DEPTH, 6 * D_MODEL // n_blk),
        in_specs=[
            pl.BlockSpec((MOD_ROWS, D_MODEL), lambda i, n: (0, 0)),
            pl.BlockSpec((None, D_MODEL, n_blk), lambda i, n: (i, 0, n)),
            pl.BlockSpec((None, 1, n_blk), lambda i, n: (i, 0, n)),
        ],
        out_specs=pl.BlockSpec((None, MOD_ROWS, n_blk), lambda i, n: (i, 0, n)),
        out_shape=jax.ShapeDtypeStruct((DEPTH, MOD_ROWS, 6 * D_MODEL), F32),
        compiler_params=_params(2),
        name="modulation",
    )(rows, ada_w, ada_b.reshape(DEPTH, 1, 6 * D_MODEL))
    return out.reshape(DEPTH * MOD_ROWS * 6, 1, D_MODEL)


def _pre_norm(x, g_ref, sc_ref, sh_ref):
    return _rms(x) * (g_ref[...] * (1.0 + sc_ref[...])) + sh_ref[...]


def _p1_even_kernel(x_ref, sh_ref, sc_ref, g_ref, w_ref, wvt_ref, rc_ref, ru_ref, rd_ref, a_ref, q_ref, k_ref, vt_ref):
    hb = _pre_norm(x_ref[...], g_ref, sc_ref, sh_ref).astype(BF16)
    a_ref[...] = _dot(hb, w_ref[:, 0:2 * A_WIDTH]).astype(BF16)
    c, s_up, s_dn = rc_ref[...], ru_ref[...], rd_ref[...]
    o = 2 * A_WIDTH
    for col, out_ref, scale in ((o, q_ref, B_HEAD_DIM ** -0.5 * LOG2_E), (o + B_QK_COLS, k_ref, 1.0)):
        y = _dot(hb, w_ref[:, col:col + B_QK_COLS])
        for j in range(B_QK_COLS // LANES):
            yj = _rope(y[:, j * LANES:(j + 1) * LANES], c, s_up, s_dn, B_HEAD_DIM // 4)
            out_ref[:, j * LANES:(j + 1) * LANES] = (yj * scale).astype(BF16)
    _store_vt(vt_ref, _dot_nt(wvt_ref[...], hb), B_HEADS, B_V_DIM)


def _p1_even(layer, xs, mods, g, w_in, rope):
    qk_cols = 2 * A_WIDTH + 2 * B_QK_COLS
    vt_rows = B_V_DIM + BF16_SUBLANES
    return pl.pallas_call(
        _p1_even_kernel,
        grid=(NTILES,),
        in_specs=[_tok_spec(D_MODEL), _mod_spec(layer, 0), _mod_spec(layer, 1), _full((1, D_MODEL)),
                  _full((D_MODEL, qk_cols)), _full((B_WIDTH, D_MODEL)), _rope_spec(), _rope_spec(), _rope_spec()],
        out_specs=[_tok_spec(2 * A_WIDTH), _tok_spec(B_QK_COLS), _tok_spec(B_QK_COLS), _vt_spec(B_HEADS, vt_rows)],
        out_shape=[jax.ShapeDtypeStruct((NT, 2 * A_WIDTH), BF16), jax.ShapeDtypeStruct((NT, B_QK_COLS), BF16),
                   jax.ShapeDtypeStruct((NT, B_QK_COLS), BF16),
                   jax.ShapeDtypeStruct((BATCH, B_HEADS, vt_rows, TOK), BF16)],
        compiler_params=_params(1),
        name="p1_even",
    )(xs, mods, mods, g, w_in[:, :qk_cols].astype(BF16), w_in[:, qk_cols:].T.astype(BF16), *rope)


def _softmax_step(st, vt, state):
    m_chunk = jnp.max(st, axis=0, keepdims=True)
    m_new = m_chunk if state is None else jnp.maximum(state[0], m_chunk)
    pv = _dot(vt, jnp.exp2((st - m_new).astype(BF16)))
    if state is None:
        return m_new, pv
    return m_new, jnp.exp2(state[0] - m_new) * state[1] + pv


def _attend(streams, lookahead, scores, values, finish):
    ctx_only = [(0, CTX_LEN)]
    everything = ctx_only + [(CTX_LEN + i * SCORE_CHUNK, SCORE_CHUNK) for i in range(SEQ // SCORE_CHUNK)]
    qi = pl.program_id(2)
    for chunks, cond in ((ctx_only, qi == 0), (everything, qi > 0)):
        items = [(start, size, s) for start, size in chunks for s in range(streams)]

        @pl.when(cond)
        def _(items=items):
            state = [None] * streams
            pending = [scores(*item) for item in items[:lookahead]]
            for n, (start, size, s) in enumerate(items):
                if n + lookahead < len(items):
                    pending.append(scores(*items[n + lookahead]))
                st = pending.pop(0)
                for j in range(0, size, KV_CHUNK):
                    state[s] = _softmax_step(st[j:j + KV_CHUNK], values(start + j, KV_CHUNK, s), state[s])
            finish([acc for _, acc in state])


def _diff_attn_kernel(lam_ref, sub_ref, q_ref, k_ref, vt_ref, o_ref, *, lambda_init):
    lane = lax.broadcasted_iota(jnp.int32, (TM, LANES), 1)
    zero = jnp.zeros((TM, LANES), BF16)
    qs = []
    for h in range(B_HEADS_PER_STEP):
        q = q_ref[:, h * LANES:(h + 1) * LANES]
        qs += [jnp.where(lane < B_HEAD_DIM, q, zero), jnp.where(lane >= B_HEAD_DIM, q, zero)]

    def scores(start, size, s):
        h = s // 2
        return _dot_nt(k_ref[start:start + size, h * LANES:(h + 1) * LANES], qs[s])

    def values(start, size, s):
        return vt_ref[s // 2, :, start:start + size]

    def finish(accs):
        lv = lam_ref[...]
        lam = (jnp.exp(jnp.sum(lv[0:1] * lv[1:2], axis=-1, keepdims=True))
               - jnp.exp(jnp.sum(lv[2:3] * lv[3:4], axis=-1, keepdims=True)) + lambda_init)
        for h in range(B_HEADS_PER_STEP):
            o1, o2 = (acc[0:B_V_DIM, :] / acc[B_V_DIM:B_V_DIM + 1, :] for acc in accs[2 * h:2 * h + 2])
            o = (o1 - lam * o2).T
            o_ref[:, h * B_V_DIM:(h + 1) * B_V_DIM] = (_rms(o) * (sub_ref[...] * (1.0 - lambda_init))).astype(BF16)

    _attend(2 * B_HEADS_PER_STEP, B_SCORE_LOOKAHEAD, scores, values, finish)


def _diff_attn(q, k, vt, lam_vecs, subln_g, lambda_init):
    rows = B_V_DIM + BF16_SUBLANES
    width = B_HEADS_PER_STEP * LANES
    return pl.pallas_call(
        functools.partial(_diff_attn_kernel, lambda_init=lambda_init),
        grid=(BATCH, B_HEADS // B_HEADS_PER_STEP, TPB),
        in_specs=[pl.BlockSpec((4, B_HEAD_DIM), lambda b, h, i: (0, 0)),
                  pl.BlockSpec((1, B_V_DIM), lambda b, h, i: (0, 0)),
                  pl.BlockSpec((None, TM, width), lambda b, h, i: (b, i, h)),
                  pl.BlockSpec((None, TOK, width), lambda b, h, i: (b, 0, h)),
                  pl.BlockSpec((None, B_HEADS_PER_STEP, rows, TOK), lambda b, h, i: (b, h, 0, 0))],
        out_specs=pl.BlockSpec((None, TM, B_HEADS_PER_STEP * B_V_DIM), lambda b, h, i: (b, i, h)),
        out_shape=jax.ShapeDtypeStruct((BATCH, TOK, B_WIDTH), BF16),
        compiler_params=_params(3),
        name="diff_attn",
    )(lam_vecs, subln_g, q, k, vt)


def _post_even_kernel(a_ref, ap_ref, an_ref, ob_ref, x_ref, gate_ref, w_ref, cw_ref, cb_ref, lg_ref, lb_ref,
                      gp_ref, o_ref, u_ref):
    prev_ok, next_ok = _halo_valid(UNIFIED)

    def glu(a):
        a = a.astype(F32)
        return a[:, :A_WIDTH] * jax.nn.sigmoid(a[:, A_WIDTH:])

    ext = TM + 2 * A_HALO
    u_ref[0, 0:A_HALO, :] = glu(ap_ref[...]) * prev_ok
    u_ref[0, A_HALO:A_HALO + TM, :] = glu(a_ref[...])
    u_ref[0, A_HALO + TM:, :] = glu(an_ref[...]) * next_ok
    for s in range(1, SUBLANES):
        u_ref[s, 0:ext - SUBLANES, :] = u_ref[0, s:s + ext - SUBLANES, :]

    rows = 32
    base = A_HALO - A_KERNEL // 2
    outs = []
    for r0 in range(0, TM, rows):
        acc = jnp.broadcast_to(cb_ref[...], (rows, A_WIDTH))
        for tap in range(A_KERNEL):
            s = (base + tap) % SUBLANES
            i0 = r0 + base + tap - s
            acc = acc + cw_ref[tap:tap + 1, :] * u_ref[s, i0:i0 + rows, :]
        mu = jnp.mean(acc, axis=-1, keepdims=True)
        cen = acc - mu
        var = jnp.mean(cen * cen, axis=-1, keepdims=True)
        y = cen * lax.rsqrt(var + LN_EPS) * lg_ref[...] + lb_ref[...]
        outs.append(_silu(y).astype(BF16))
    ya = jnp.concatenate(outs, axis=0)
    y = _dot(ya, w_ref[0:A_WIDTH, :]) + _dot(ob_ref[...], w_ref[A_WIDTH:, :])
    o_ref[...] = x_ref[...] + gate_ref[...] * (_rms(y) * gp_ref[...])


def _post_even(layer, a, ob, xs, mods, w_out, conv_w, conv_b, ln_g, ln_b, g_post):
    return pl.pallas_call(
        _post_even_kernel,
        grid=(NTILES,),
        in_specs=[_tok_spec(2 * A_WIDTH), *_halo_specs(2 * A_WIDTH, A_HALO),
                  _tok_spec(B_WIDTH), _tok_spec(D_MODEL), _mod_spec(layer, 2),
                  _full((A_WIDTH + B_WIDTH, D_MODEL)), _full((A_KERNEL + 1, A_WIDTH)), _full((1, A_WIDTH)),
                  _full((1, A_WIDTH)), _full((1, A_WIDTH)), _full((1, D_MODEL))],
        out_specs=_tok_spec(D_MODEL),
        out_shape=jax.ShapeDtypeStruct((NT, D_MODEL), F32),
        scratch_shapes=[pltpu.VMEM((SUBLANES, TM + 2 * A_HALO, A_WIDTH), F32)],
        compiler_params=_params(1),
        name="post_even",
    )(a, a, a, ob, xs, mods, w_out, conv_w, conv_b, ln_g, ln_b, g_post)


def _p1_odd_kernel(x_ref, sh_ref, sc_ref, g_ref, w_ref, gq_ref, gkv_ref, wq_ref, wk_ref, wvt_ref,
                   rc_ref, ru_ref, rd_ref, q_ref, k_ref, vt_ref):
    hb = _pre_norm(x_ref[...], g_ref, sc_ref, sh_ref).astype(BF16)
    p = _dot(hb, w_ref[...])
    c, s_up, s_dn = rc_ref[...], ru_ref[...], rd_ref[...]
    shift = C_ROPE // 4
    scale = (C_NOPE + C_ROPE) ** -0.5 * LOG2_E

    cq =(_rms(p[:, :C_Q_RANK]) * gq_ref[...]).astype(BF16)
    q = _dot(cq, wq_ref[...])
    for j in range(C_HEADS):
        sl = slice(j * HEAD_PAD, (j + 1) * HEAD_PAD)
        q_ref[:, sl] = (_rope(q[:, sl], c, s_up, s_dn, shift) * scale).astype(BF16)

    ckv = (_rms(p[:, C_Q_RANK:C_Q_RANK + C_KV_RANK]) * gkv_ref[...]).astype(BF16)
    k_pe = _rope(p[:, C_Q_RANK + C_KV_RANK:], c, s_up, s_dn, shift)
    k = _dot(ckv, wk_ref[...])
    for j in range(C_HEADS):
        sl = slice(j * HEAD_PAD, (j + 1) * HEAD_PAD)
        k_ref[:, sl] = (k[:, sl] + k_pe).astype(BF16)
    _store_vt(vt_ref, _dot_nt(wvt_ref[...], ckv), C_HEADS, C_V)


def _p1_odd(layer, xs, mods, g, w_in, g_q, g_kv, w_q, w_k, w_vt, rope):
    width = C_HEADS * HEAD_PAD
    v_width = C_HEADS * C_V
    vt_rows = C_V + BF16_SUBLANES
    in_cols = C_Q_RANK + C_KV_RANK + HEAD_PAD
    out = jax.ShapeDtypeStruct((NT, width), BF16)
    return pl.pallas_call(
        _p1_odd_kernel,
        grid=(NTILES,),
        in_specs=[_tok_spec(D_MODEL), _mod_spec(layer, 0), _mod_spec(layer, 1), _full((1, D_MODEL)),
                  _full((D_MODEL, in_cols)), _full((1, C_Q_RANK)), _full((1, C_KV_RANK)),
                  _full((C_Q_RANK, width)), _full((C_KV_RANK, width)), _full((v_width, C_KV_RANK)),
                  _rope_spec(), _rope_spec(), _rope_spec()],
        out_specs=[_tok_spec(width), _tok_spec(width), _vt_spec(C_HEADS, vt_rows)],
        out_shape=[out, out, jax.ShapeDtypeStruct((BATCH, C_HEADS, vt_rows, TOK), BF16)],
        compiler_params=_params(1),
        name="p1_odd",
    )(xs, mods, mods, g, w_in, g_q, g_kv, w_q, w_k, w_vt, *rope)


def _mla_kernel(q_ref, k_ref, vt_ref, o_ref):
    def scores(start, size, h):
        sl = slice(h * HEAD_PAD, (h + 1) * HEAD_PAD)
        return _dot_nt(k_ref[start:start + size, sl], q_ref[:, sl])

    def values(start, size, h):
        return vt_ref[h, :, start:start + size]

    def finish(accs):
        o_t = [acc[0:C_V, :] / acc[C_V:C_V + 1, :] for acc in accs]
        o_ref[...] = jnp.concatenate(o_t, axis=0).T.astype(BF16)

    _attend(C_HEADS_PER_STEP, C_SCORE_LOOKAHEAD, scores, values, finish)


def _mla_attn(q, k, vt):
    width = C_HEADS_PER_STEP * HEAD_PAD
    rows = C_V + BF16_SUBLANES
    return pl.pallas_call(
        _mla_kernel,
        grid=(BATCH, C_HEADS // C_HEADS_PER_STEP, TPB),
        in_specs=[pl.BlockSpec((None, TM, width), lambda b, h, i: (b, i, h)),
                  pl.BlockSpec((None, TOK, width), lambda b, h, i: (b, 0, h)),
                  pl.BlockSpec((None, C_HEADS_PER_STEP, rows, TOK), lambda b, h, i: (b, h, 0, 0))],
        out_specs=pl.BlockSpec((None, TM, C_HEADS_PER_STEP * C_V), lambda b, h, i: (b, i, h)),
        out_shape=jax.ShapeDtypeStruct((BATCH, TOK, C_HEADS * C_V), BF16),
        compiler_params=_params(3),
        name="mla_attn",
    )(q, k, vt)


def _post_odd_kernel(o_ref, x_ref, gate_ref, w_ref, gp_ref, out_ref):
    y = _dot(o_ref[...], w_ref[...])
    out_ref[...] = x_ref[...] + gate_ref[...] * (_rms(y) * gp_ref[...])


def _post_odd(layer, o, xs, mods, w_out, g_post, stream):
    width = C_HEADS * C_V
    in_spec = _tok_spec if stream.has_ctx else _latent_of_unified_spec
    return pl.pallas_call(
        _post_odd_kernel,
        grid=(stream.tiles,),
        in_specs=[in_spec(width), in_spec(D_MODEL), _mod_spec(layer, 2, stream), _full((width, D_MODEL)),
                  _full((1, D_MODEL))],
        out_specs=_tok_spec(D_MODEL),
        out_shape=jax.ShapeDtypeStruct((stream.rows, D_MODEL), F32),
        compiler_params=_params(1),
        name="post_odd",
    )(o, xs, mods, w_out, g_post)


def _ffn_tile(x, x_prev, x_next, prev_ok, next_ok, sh_ref, sc_ref, gate_ref, g_ref, gp_ref, wu_ref, cw_ref, cb_ref,
              wd_ref, g_scr, act_scr):
    h = _pre_norm(x, g_ref, sc_ref, sh_ref)
    h_prev = _pre_norm(x_prev, g_ref, sc_ref, sh_ref)
    h_next = _pre_norm(x_next, g_ref, sc_ref, sh_ref)
    hb = h.astype(BF16)
    hb_ext = jnp.concatenate([h_prev, h, h_next], axis=0).astype(BF16)
    lo, hi = F_HALO, F_HALO + TM
    for c in range(FFN_HIDDEN // FFN_CHUNK):
        cols = slice(c * FFN_CHUNK, (c + 1) * FFN_CHUNK)
        gate = _dot(hb_ext, wu_ref[:, FFN_HIDDEN + c * FFN_CHUNK:FFN_HIDDEN + (c + 1) * FFN_CHUNK])
        g_scr[0:lo, :] = gate[0:lo] * prev_ok
        g_scr[lo:hi, :] = gate[lo:hi]
        g_scr[hi:, :] = gate[hi:] * next_ok
        conv = (cw_ref[0:1, cols] * g_scr[lo - 1:hi - 1, :] + cw_ref[1:2, cols] * gate[lo:hi]
                + cw_ref[2:3, cols] * g_scr[lo + 1:hi + 1, :] + cb_ref[:, cols])
        val = _dot(hb, wu_ref[:, cols])
        act_scr[:, cols] = (_silu(conv) * val).astype(BF16)
    y = _dot(act_scr[...], wd_ref[...])
    return x + gate_ref[...] * (_rms(y) * gp_ref[...])


def _ffn_kernel(x_ref, xp_ref, xn_ref, *refs, stream):
    mods = refs[:3 * FFN_TILES]
    g_ref, gp_ref, wu_ref, cw_ref, cb_ref, wd_ref, o_ref, g_scr, act_scr = refs[3 * FFN_TILES:]
    for s in range(FFN_TILES):
        rows = slice(s * TM, (s + 1) * TM)
        x_prev = xp_ref[...] if s == 0 else x_ref[s * TM - F_HALO:s * TM, :]
        x_next = xn_ref[...] if s == FFN_TILES - 1 else x_ref[(s + 1) * TM:(s + 1) * TM + F_HALO, :]
        prev_ok, next_ok = _halo_valid(stream, pl.program_id(0) * FFN_TILES + s)
        o_ref[rows, :] = _ffn_tile(x_ref[rows, :], x_prev, x_next, prev_ok, next_ok, *mods[3 * s:3 * s + 3],
                                   g_ref, gp_ref, wu_ref, cw_ref, cb_ref, wd_ref, g_scr.at[s], act_scr.at[s])


def _ffn(layer, xs, mods, g_pre, g_post, w_up, conv_w, conv_b, w_down, stream):
    mod_specs = [_mod_spec(layer, slot, stream, s, FFN_TILES) for s in range(FFN_TILES) for slot in (3, 4, 5)]
    return pl.pallas_call(
        functools.partial(_ffn_kernel, stream=stream),
        grid=(stream.tiles // FFN_TILES,),
        in_specs=[_tok_spec(D_MODEL, FFN_TILES), *_halo_specs(D_MODEL, F_HALO, stream, FFN_TILES), *mod_specs,
                  _full((1, D_MODEL)), _full((1, D_MODEL)), _full((D_MODEL, 2 * FFN_HIDDEN)),
                  _full((SUBLANES, FFN_HIDDEN)), _full((1, FFN_HIDDEN)), _full((FFN_HIDDEN, D_MODEL))],
        out_specs=_tok_spec(D_MODEL, FFN_TILES),
        out_shape=jax.ShapeDtypeStruct((stream.rows, D_MODEL), F32),
        scratch_shapes=[pltpu.VMEM((FFN_TILES, TM + 2 * F_HALO, FFN_CHUNK), F32),
                        pltpu.VMEM((FFN_TILES, TM, FFN_HIDDEN), BF16)],
        compiler_params=_params(1),
        name="conv_ffn",
    )(xs, xs, xs, *([mods] * len(mod_specs)), g_pre, g_post, w_up, conv_w, conv_b, w_down)


def _pad_rows(w, rows):
    return jnp.concatenate([w, jnp.zeros((rows - w.shape[0],) + w.shape[1:], w.dtype)], axis=0)


def _pad_last(w, width):
    return jnp.concatenate([w, jnp.zeros(w.shape[:-1] + (width - w.shape[-1],), w.dtype)], axis=-1)


def _mla_weights(w_in, w_uq, w_ukv, w_out):
    rank = C_Q_RANK + C_KV_RANK
    pe = jnp.concatenate([jnp.zeros((D_MODEL, C_NOPE), F32), w_in[:, rank:],
                          jnp.zeros((D_MODEL, HEAD_PAD - C_NOPE - C_ROPE), F32)], axis=1)
    w_in_p = jnp.concatenate([w_in[:, :rank], pe], axis=1)
    w_q = _pad_last(w_uq.reshape(C_Q_RANK, C_HEADS, C_NOPE + C_ROPE), HEAD_PAD).reshape(C_Q_RANK, -1)
    kv = w_ukv.reshape(C_KV_RANK, C_HEADS, C_NOPE + C_V)
    w_k = _pad_last(kv[..., :C_NOPE], HEAD_PAD).reshape(C_KV_RANK, -1)
    w_vt = kv[..., C_NOPE:].reshape(C_KV_RANK, -1).T
    return tuple(w.astype(BF16) for w in (w_in_p, w_q, w_k, w_vt, w_out))


def kernel(x, c, ctx, c_ctx, ada_w, ada_b, norm_mix_pre, norm_mix_post, norm_ffn_pre, norm_ffn_post, ffn_w_up,
           ffn_conv_w, ffn_conv_b, ffn_w_down, ab_w_in, a_conv_w, a_conv_b, a_ln_g, a_ln_b, b_lambda, b_subln,
           ab_w_out, c_w_in, c_q_norm, c_kv_norm, c_w_uq, c_w_ukv, c_w_out):
    mods = _modulation(c, c_ctx, ada_w, ada_b)
    xs = jnp.concatenate([ctx, x], axis=1).reshape(NT, D_MODEL)
    rope_b = _rope_tables(B_HEAD_DIM, 0, B_HEAD_DIM)
    rope_c = _rope_tables(C_ROPE, C_NOPE, HEAD_PAD)
    row = lambda v: v.reshape(1, -1)
    tokens = lambda t: t.reshape(BATCH, TOK, t.shape[-1])
    flat = lambda t: t.reshape(NT, t.shape[-1])

    for i in range(DEPTH):
        j = i // 2
        stream = UNIFIED if i < DEPTH - 1 else LATENT
        if i % 2 == 0:
            lambda_init = 0.8 - 0.6 * math.exp(-0.3 * i)
            a, q, k, vt = _p1_even(i, xs, mods, row(norm_mix_pre[i]), ab_w_in[j], rope_b)
            ob = _diff_attn(tokens(q), tokens(k), vt, b_lambda[j], row(b_subln[j]), lambda_init)
            xs = _post_even(i, a, flat(ob), xs, mods, ab_w_out[j].astype(BF16),
                            _pad_rows(a_conv_w[j], A_KERNEL + 1), row(a_conv_b[j]), row(a_ln_g[j]),
                            row(a_ln_b[j]), row(norm_mix_post[i]))
        else:
            w_in, w_q, w_k, w_vt, w_o = _mla_weights(c_w_in[j], c_w_uq[j], c_w_ukv[j], c_w_out[j])
            q, k, vt = _p1_odd(i, xs, mods, row(norm_mix_pre[i]), w_in, row(c_q_norm[j]), row(c_kv_norm[j]),
                               w_q, w_k, w_vt, rope_c)
            o = _mla_attn(tokens(q), tokens(k), vt)
            xs = _post_odd(i, flat(o), xs, mods, w_o, row(norm_mix_post[i]), stream)
        xs = _ffn(i, xs, mods, row(norm_ffn_pre[i]), row(norm_ffn_post[i]), ffn_w_up[i].astype(BF16),
                  _pad_rows(ffn_conv_w[i], SUBLANES), row(ffn_conv_b[i]), ffn_w_down[i].astype(BF16), stream)
    return xs.reshape(BATCH, SEQ, D_MODEL)
```

```python
import functools
import math
from typing import NamedTuple

import jax
import jax.numpy as jnp
from jax import lax
from jax.experimental import pallas as pl
from jax.experimental.pallas import tpu as pltpu

D_MODEL = 1024
BATCH = 8
SEQ = 4096
DEPTH = 4
CTX_LEN = 256
GRID_W = 64
ROPE_BASE = 10000.0
RMS_EPS = 1e-6
LN_EPS = 1e-5

A_WIDTH = 512
A_KERNEL = 31
B_HEADS = 4
B_HEAD_DIM = 64
B_V_DIM = 2 * B_HEAD_DIM
B_QK_COLS = B_HEADS * 2 * B_HEAD_DIM
B_WIDTH = B_HEADS * B_V_DIM
AB_IN = 2 * A_WIDTH + 2 * B_QK_COLS + B_WIDTH

C_HEADS = 16
C_NOPE = 64
C_ROPE = 32
C_V = 64
C_Q_RANK = 768
C_KV_RANK = 256

FFN_HIDDEN = 2816
FFN_KERNEL = 3

LANES = 128
SUBLANES = 8
BF16_SUBLANES = 16
MXU_COLS = 256
VMEM_LIMIT_BYTES = 56 * 1024 * 1024

TOK = CTX_LEN + SEQ
NT = BATCH * TOK
TM = 256
TPB = TOK // TM
NTILES = NT // TM
MOD_ROWS = 16
CTX_ROW = BATCH
MIX_SLOT, MIX_GATE, FFN_SLOT, FFN_GATE = 0, 2, 3, 5
HEAD_PAD = LANES
C_HEADS_PER_STEP = 8
B_HEADS_PER_STEP = 4
SCORE_CHUNK = 1024
KV_CHUNK = 256
B_SCORE_LOOKAHEAD = 3
C_SCORE_LOOKAHEAD = 2
FFN_CHUNK = MXU_COLS
A_HALO = 16
F_HALO = SUBLANES

F32 = jnp.float32
BF16 = jnp.bfloat16
LOG2_E = math.log2(math.e)


def _dot(a, b):
    return jnp.dot(a, b, preferred_element_type=F32)


def _dot_nt(a, b):
    return lax.dot_general(a, b, (((1,), (1,)), ((), ())), preferred_element_type=F32)


def _rms(x, eps=RMS_EPS):
    return x * lax.rsqrt(jnp.mean(x * x, axis=-1, keepdims=True) + eps)


def _silu(x):
    return x * jax.nn.sigmoid(x)


def _params(n_axes):
    return pltpu.CompilerParams(dimension_semantics=("parallel",) * n_axes, vmem_limit_bytes=VMEM_LIMIT_BYTES)


def _full(shape):
    return pl.BlockSpec(shape, lambda *_: (0,) * len(shape), pipeline_mode=pl.Buffered(1))


def _tile_in_batch(t):
    return t % TPB


class _Stream(NamedTuple):
    tiles_per_batch: int
    has_ctx: bool

    @property
    def tiles(self):
        return BATCH * self.tiles_per_batch

    @property
    def rows(self):
        return self.tiles * TM


UNIFIED = _Stream(TPB, True)
LATENT = _Stream(TPB - 1, False)
assert DEPTH % 2 == 0, "only the odd-layer tail has a latent-only variant"


def _mod_spec(layer, stream=UNIFIED):
    def index(t):
        row = t // stream.tiles_per_batch
        if stream.has_ctx:
            row = jnp.where(t % stream.tiles_per_batch == 0, CTX_ROW, row)
        return (layer * MOD_ROWS + row, 0, 0)
    return pl.BlockSpec((None, 6, D_MODEL), index)


def _tok_spec(width):
    return pl.BlockSpec((TM, width), lambda t: (t, 0))


def _latent_of_unified_spec(width):
    return pl.BlockSpec((TM, width), lambda t: ((t // LATENT.tiles_per_batch) * TPB + t % LATENT.tiles_per_batch + 1, 0))


def _halo_specs(width, rows, stream=UNIFIED):
    blocks = TM // rows
    prev = lambda t: (jnp.maximum(t * blocks - 1, 0), 0)
    nxt = lambda t: (jnp.minimum((t + 1) * blocks, stream.rows // rows - 1), 0)
    return [pl.BlockSpec((rows, width), prev), pl.BlockSpec((rows, width), nxt)]


def _halo_valid(stream):
    r = pl.program_id(0) % stream.tiles_per_batch
    first_latent = 1 if stream.has_ctx else 0
    return ((r > first_latent).astype(F32),
            ((r >= first_latent) & (r < stream.tiles_per_batch - 1)).astype(F32))


def _vt_spec(heads, rows):
    return pl.BlockSpec((None, heads, rows, TM), lambda t: (t // TPB, 0, 0, _tile_in_batch(t)))


def _store_vt(vt_ref, vt, heads, dim):
    for h in range(heads):
        vt_ref[h, 0:dim, :] = vt[h * dim:(h + 1) * dim, :].astype(BF16)
        vt_ref[h, dim:dim + BF16_SUBLANES, :] = jnp.ones((BF16_SUBLANES, TM), BF16)


def _rope_spec():
    return pl.BlockSpec((TM, LANES), lambda t: (_tile_in_batch(t), 0))


def _rope(y, c, s_up, s_dn, shift):
    return y * c + pltpu.roll(y, LANES - shift, 1) * s_up + pltpu.roll(y, shift, 1) * s_dn


def _rope_tables(group, lane_offset, period):
    half = group // 4
    lane = jnp.arange(LANES)
    rel = (lane % period) - lane_offset
    active = (rel >= 0) & (rel < group)
    rel = jnp.where(active, rel, 0)
    by_col = rel >= group // 2
    within = rel % (group // 2)
    freq_idx = within % half
    inv_freq = ROPE_BASE ** (-freq_idx.astype(F32) / half)
    t = jnp.arange(SEQ, dtype=jnp.int32)
    pos = jnp.where(by_col[None, :], (t % GRID_W)[:, None], (t // GRID_W)[:, None]).astype(F32)
    ang = pos * inv_freq[None, :]
    cos, sin = jnp.cos(ang), jnp.sin(ang)
    first = within < half
    c = jnp.where(active[None, :], cos, 1.0)
    s_up = jnp.where((active & first)[None, :], -sin, 0.0)
    s_dn = jnp.where((active & ~first)[None, :], sin, 0.0)
    ident = [jnp.ones((CTX_LEN, LANES), F32), jnp.zeros((CTX_LEN, LANES), F32), jnp.zeros((CTX_LEN, LANES), F32)]
    return tuple(jnp.concatenate([i, x.astype(F32)], axis=0) for i, x in zip(ident, (c, s_up, s_dn)))


def _mods_kernel(c_ref, w_ref, b_ref, o_ref):
    s = _silu(c_ref[...])
    o_ref[...] = jnp.dot(s, w_ref[...], preferred_element_type=F32, precision=lax.Precision.HIGHEST) + b_ref[...]


def _modulation(c, c_ctx, ada_w, ada_b):
    rows = jnp.concatenate([c, c_ctx[None, :], jnp.zeros((MOD_ROWS - BATCH - 1, D_MODEL), F32)], axis=0)
    n_blk = 1536
    out = pl.pallas_call(
        _mods_kernel,
        grid=(DEPTH, 6 * D_MODEL // n_blk),
        in_specs=[
            pl.BlockSpec((MOD_ROWS, D_MODEL), lambda i, n: (0, 0)),
            pl.BlockSpec((None, D_MODEL, n_blk), lambda i, n: (i, 0, n)),
            pl.BlockSpec((None, 1, n_blk), lambda i, n: (i, 0, n)),
        ],
        out_specs=pl.BlockSpec((None, MOD_ROWS, n_blk), lambda i, n: (i, 0, n)),
        out_shape=jax.ShapeDtypeStruct((DEPTH, MOD_ROWS, 6 * D_MODEL), F32),
        compiler_params=_params(2),
        name="modulation",
    )(rows, ada_w, ada_b.reshape(DEPTH, 1, 6 * D_MODEL))
    return out.reshape(DEPTH * MOD_ROWS, 6, D_MODEL)


def _pre_norm(x, g_ref, mod_ref, slot):
    return _rms(x) * (g_ref[...] * (1.0 + mod_ref[slot + 1:slot + 2, :])) + mod_ref[slot:slot + 1, :]


def _p1_even_kernel(x_ref, mod_ref, g_ref, w_ref, wvt_ref, rc_ref, ru_ref, rd_ref, a_ref, q_ref, k_ref, vt_ref):
    hb = _pre_norm(x_ref[...], g_ref, mod_ref, MIX_SLOT).astype(BF16)
    a_ref[...] = _dot(hb, w_ref[:, 0:2 * A_WIDTH]).astype(BF16)
    c, s_up, s_dn = rc_ref[...], ru_ref[...], rd_ref[...]
    o = 2 * A_WIDTH
    for col, out_ref, scale in ((o, q_ref, B_HEAD_DIM ** -0.5 * LOG2_E), (o + B_QK_COLS, k_ref, 1.0)):
        y = _dot(hb, w_ref[:, col:col + B_QK_COLS])
        for j in range(B_QK_COLS // LANES):
            yj = _rope(y[:, j * LANES:(j + 1) * LANES], c, s_up, s_dn, B_HEAD_DIM // 4)
            out_ref[:, j * LANES:(j + 1) * LANES] = (yj * scale).astype(BF16)
    _store_vt(vt_ref, _dot_nt(wvt_ref[...], hb), B_HEADS, B_V_DIM)


def _p1_even(layer, xs, mods, g, w_in, rope):
    qk_cols = 2 * A_WIDTH + 2 * B_QK_COLS
    vt_rows = B_V_DIM + BF16_SUBLANES
    return pl.pallas_call(
        _p1_even_kernel,
        grid=(NTILES,),
        in_specs=[_tok_spec(D_MODEL), _mod_spec(layer), _full((1, D_MODEL)),
                  _full((D_MODEL, qk_cols)), _full((B_WIDTH, D_MODEL)), _rope_spec(), _rope_spec(), _rope_spec()],
        out_specs=[_tok_spec(2 * A_WIDTH), _tok_spec(B_QK_COLS), _tok_spec(B_QK_COLS), _vt_spec(B_HEADS, vt_rows)],
        out_shape=[jax.ShapeDtypeStruct((NT, 2 * A_WIDTH), BF16), jax.ShapeDtypeStruct((NT, B_QK_COLS), BF16),
                   jax.ShapeDtypeStruct((NT, B_QK_COLS), BF16),
                   jax.ShapeDtypeStruct((BATCH, B_HEADS, vt_rows, TOK), BF16)],
        compiler_params=_params(1),
        name="p1_even",
    )(xs, mods, g, w_in[:, :qk_cols].astype(BF16), w_in[:, qk_cols:].T.astype(BF16), *rope)


def _softmax_step(st, vt, state):
    m_chunk = jnp.max(st, axis=0, keepdims=True)
    m_new = m_chunk if state is None else jnp.maximum(state[0], m_chunk)
    pv = _dot(vt, jnp.exp2((st - m_new).astype(BF16)))
    if state is None:
        return m_new, pv
    return m_new, jnp.exp2(state[0] - m_new) * state[1] + pv


def _attend(streams, lookahead, scores, values, finish):
    ctx_only = [(0, CTX_LEN)]
    everything = ctx_only + [(CTX_LEN + i * SCORE_CHUNK, SCORE_CHUNK) for i in range(SEQ // SCORE_CHUNK)]
    qi = pl.program_id(2)
    for chunks, cond in ((ctx_only, qi == 0), (everything, qi > 0)):
        items = [(start, size, s) for start, size in chunks for s in range(streams)]

        @pl.when(cond)
        def _(items=items):
            state = [None] * streams
            pending = [scores(*item) for item in items[:lookahead]]
            for n, (start, size, s) in enumerate(items):
                if n + lookahead < len(items):
                    pending.append(scores(*items[n + lookahead]))
                st = pending.pop(0)
                step = min(KV_CHUNK, size)
                for j in range(0, size, step):
                    state[s] = _softmax_step(st[j:j + step], values(start + j, step, s), state[s])
            finish([acc for _, acc in state])


def _diff_attn_kernel(lam_ref, sub_ref, q_ref, k_ref, vt_ref, o_ref, *, lambda_init):
    lane = lax.broadcasted_iota(jnp.int32, (TM, LANES), 1)
    zero = jnp.zeros((TM, LANES), BF16)
    qs = []
    for h in range(B_HEADS_PER_STEP):
        q = q_ref[:, h * LANES:(h + 1) * LANES]
        qs += [jnp.where(lane < B_HEAD_DIM, q, zero), jnp.where(lane >= B_HEAD_DIM, q, zero)]

    def scores(start, size, s):
        h = s // 2
        return _dot_nt(k_ref[start:start + size, h * LANES:(h + 1) * LANES], qs[s])

    def values(start, size, s):
        return vt_ref[s // 2, :, start:start + size]

    def finish(accs):
        lv = lam_ref[...]
        lam = (jnp.exp(jnp.sum(lv[0:1] * lv[1:2], axis=-1, keepdims=True))
               - jnp.exp(jnp.sum(lv[2:3] * lv[3:4], axis=-1, keepdims=True)) + lambda_init)
        for h in range(B_HEADS_PER_STEP):
            o1, o2 = (acc[0:B_V_DIM, :] / acc[B_V_DIM:B_V_DIM + 1, :] for acc in accs[2 * h:2 * h + 2])
            o = (o1 - lam * o2).T
            o_ref[:, h * B_V_DIM:(h + 1) * B_V_DIM] = (_rms(o) * (sub_ref[...] * (1.0 - lambda_init))).astype(BF16)

    _attend(2 * B_HEADS_PER_STEP, B_SCORE_LOOKAHEAD, scores, values, finish)


def _diff_attn(q, k, vt, lam_vecs, subln_g, lambda_init):
    rows = B_V_DIM + BF16_SUBLANES
    width = B_HEADS_PER_STEP * LANES
    return pl.pallas_call(
        functools.partial(_diff_attn_kernel, lambda_init=lambda_init),
        grid=(BATCH, B_HEADS // B_HEADS_PER_STEP, TPB),
        in_specs=[pl.BlockSpec((4, B_HEAD_DIM), lambda b, h, i: (0, 0)),
                  pl.BlockSpec((1, B_V_DIM), lambda b, h, i: (0, 0)),
                  pl.BlockSpec((None, TM, width), lambda b, h, i: (b, i, h)),
                  pl.BlockSpec((None, TOK, width), lambda b, h, i: (b, 0, h)),
                  pl.BlockSpec((None, B_HEADS_PER_STEP, rows, TOK), lambda b, h, i: (b, h, 0, 0))],
        out_specs=pl.BlockSpec((None, TM, B_HEADS_PER_STEP * B_V_DIM), lambda b, h, i: (b, i, h)),
        out_shape=jax.ShapeDtypeStruct((BATCH, TOK, B_WIDTH), BF16),
        compiler_params=_params(3),
        name="diff_attn",
    )(lam_vecs, subln_g, q, k, vt)


def _post_even_kernel(a_ref, ap_ref, an_ref, ob_ref, x_ref, mod_ref, w_ref, cw_ref, cb_ref, lg_ref, lb_ref,
                      gp_ref, o_ref, u_ref):
    prev_ok, next_ok = _halo_valid(UNIFIED)

    def glu(a):
        a = a.astype(F32)
        return a[:, :A_WIDTH] * jax.nn.sigmoid(a[:, A_WIDTH:])

    ext = TM + 2 * A_HALO
    u_ref[0, 0:A_HALO, :] = glu(ap_ref[...]) * prev_ok
    u_ref[0, A_HALO:A_HALO + TM, :] = glu(a_ref[...])
    u_ref[0, A_HALO + TM:, :] = glu(an_ref[...]) * next_ok
    for s in range(1, SUBLANES):
        u_ref[s, 0:ext - SUBLANES, :] = u_ref[0, s:s + ext - SUBLANES, :]

    rows = 32
    base = A_HALO - A_KERNEL // 2
    outs = []
    for r0 in range(0, TM, rows):
        acc = jnp.broadcast_to(cb_ref[...], (rows, A_WIDTH))
        for tap in range(A_KERNEL):
            s = (base + tap) % SUBLANES
            i0 = r0 + base + tap - s
            acc = acc + cw_ref[tap:tap + 1, :] * u_ref[s, i0:i0 + rows, :]
        mu = jnp.mean(acc, axis=-1, keepdims=True)
        cen = acc - mu
        var = jnp.mean(cen * cen, axis=-1, keepdims=True)
        y = cen * lax.rsqrt(var + LN_EPS) * lg_ref[...] + lb_ref[...]
        outs.append(_silu(y).astype(BF16))
    ya = jnp.concatenate(outs, axis=0)
    y = _dot(ya, w_ref[0:A_WIDTH, :]) + _dot(ob_ref[...], w_ref[A_WIDTH:, :])
    o_ref[...] = x_ref[...] + mod_ref[MIX_GATE:MIX_GATE + 1, :] * (_rms(y) * gp_ref[...])


def _post_even(layer, a, ob, xs, mods, w_out, conv_w, conv_b, ln_g, ln_b, g_post):
    return pl.pallas_call(
        _post_even_kernel,
        grid=(NTILES,),
        in_specs=[_tok_spec(2 * A_WIDTH), *_halo_specs(2 * A_WIDTH, A_HALO),
                  _tok_spec(B_WIDTH), _tok_spec(D_MODEL), _mod_spec(layer),
                  _full((A_WIDTH + B_WIDTH, D_MODEL)), _full((A_KERNEL + 1, A_WIDTH)), _full((1, A_WIDTH)),
                  _full((1, A_WIDTH)), _full((1, A_WIDTH)), _full((1, D_MODEL))],
        out_specs=_tok_spec(D_MODEL),
        out_shape=jax.ShapeDtypeStruct((NT, D_MODEL), F32),
        scratch_shapes=[pltpu.VMEM((SUBLANES, TM + 2 * A_HALO, A_WIDTH), F32)],
        compiler_params=_params(1),
        name="post_even",
    )(a, a, a, ob, xs, mods, w_out, conv_w, conv_b, ln_g, ln_b, g_post)


def _p1_odd_kernel(x_ref, mod_ref, g_ref, w_ref, gq_ref, gkv_ref, wq_ref, wk_ref, wvt_ref,
                   rc_ref, ru_ref, rd_ref, q_ref, k_ref, vt_ref):
    hb = _pre_norm(x_ref[...], g_ref, mod_ref, MIX_SLOT).astype(BF16)
    p = _dot(hb, w_ref[...])
    c, s_up, s_dn = rc_ref[...], ru_ref[...], rd_ref[...]
    shift = C_ROPE // 4
    scale = (C_NOPE + C_ROPE) ** -0.5 * LOG2_E

    cq = (_rms(p[:, :C_Q_RANK]) * gq_ref[...]).astype(BF16)
    q = _dot(cq, wq_ref[...])
    for j in range(C_HEADS):
        sl = slice(j * HEAD_PAD, (j + 1) * HEAD_PAD)
        q_ref[:, sl] = (_rope(q[:, sl], c, s_up, s_dn, shift) * scale).astype(BF16)

    ckv = (_rms(p[:, C_Q_RANK:C_Q_RANK + C_KV_RANK]) * gkv_ref[...]).astype(BF16)
    k_pe = _rope(p[:, C_Q_RANK + C_KV_RANK:], c, s_up, s_dn, shift)
    k = _dot(ckv, wk_ref[...])
    for j in range(C_HEADS):
        sl = slice(j * HEAD_PAD, (j + 1) * HEAD_PAD)
        k_ref[:, sl] = (k[:, sl] + k_pe).astype(BF16)
    _store_vt(vt_ref, _dot_nt(wvt_ref[...], ckv), C_HEADS, C_V)


def _p1_odd(layer, xs, mods, g, w_in, g_q, g_kv, w_q, w_k, w_vt, rope):
    width = C_HEADS * HEAD_PAD
    v_width = C_HEADS * C_V
    vt_rows = C_V + BF16_SUBLANES
    in_cols = C_Q_RANK + C_KV_RANK + HEAD_PAD
    out = jax.ShapeDtypeStruct((NT, width), BF16)
    return pl.pallas_call(
        _p1_odd_kernel,
        grid=(NTILES,),
        in_specs=[_tok_spec(D_MODEL), _mod_spec(layer), _full((1, D_MODEL)),
                  _full((D_MODEL, in_cols)), _full((1, C_Q_RANK)), _full((1, C_KV_RANK)),
                  _full((C_Q_RANK, width)), _full((C_KV_RANK, width)), _full((v_width, C_KV_RANK)),
                  _rope_spec(), _rope_spec(), _rope_spec()],
        out_specs=[_tok_spec(width), _tok_spec(width), _vt_spec(C_HEADS, vt_rows)],
        out_shape=[out, out, jax.ShapeDtypeStruct((BATCH, C_HEADS, vt_rows, TOK), BF16)],
        compiler_params=_params(1),
        name="p1_odd",
    )(xs, mods, g, w_in, g_q, g_kv, w_q, w_k, w_vt, *rope)


def _mla_kernel(q_ref, k_ref, vt_ref, o_ref):
    def scores(start, size, h):
        sl = slice(h * HEAD_PAD, (h + 1) * HEAD_PAD)
        return _dot_nt(k_ref[start:start + size, sl], q_ref[:, sl])

    def values(start, size, h):
        return vt_ref[h, :, start:start + size]

    def finish(accs):
        o_t = [acc[0:C_V, :] / acc[C_V:C_V + 1, :] for acc in accs]
        o_ref[...] = jnp.concatenate(o_t, axis=0).T.astype(BF16)

    _attend(C_HEADS_PER_STEP, C_SCORE_LOOKAHEAD, scores, values, finish)


def _mla_attn(q, k, vt):
    width = C_HEADS_PER_STEP * HEAD_PAD
    rows = C_V + BF16_SUBLANES
    return pl.pallas_call(
        _mla_kernel,
        grid=(BATCH, C_HEADS // C_HEADS_PER_STEP, TPB),
        in_specs=[pl.BlockSpec((None, TM, width), lambda b, h, i: (b, i, h)),
                  pl.BlockSpec((None, TOK, width), lambda b, h, i: (b, 0, h)),
                  pl.BlockSpec((None, C_HEADS_PER_STEP, rows, TOK), lambda b, h, i: (b, h, 0, 0))],
        out_specs=pl.BlockSpec((None, TM, C_HEADS_PER_STEP * C_V), lambda b, h, i: (b, i, h)),
        out_shape=jax.ShapeDtypeStruct((BATCH, TOK, C_HEADS * C_V), BF16),
        compiler_params=_params(3),
        name="mla_attn",
    )(q, k, vt)


def _post_odd_kernel(o_ref, x_ref, mod_ref, w_ref, gp_ref, out_ref):
    y = _dot(o_ref[...], w_ref[...])
    out_ref[...] = x_ref[...] + mod_ref[MIX_GATE:MIX_GATE + 1, :] * (_rms(y) * gp_ref[...])


def _post_odd(layer, o, xs, mods, w_out, g_post, stream):
    width = C_HEADS * C_V
    in_spec = _tok_spec if stream.has_ctx else _latent_of_unified_spec
    return pl.pallas_call(
        _post_odd_kernel,
        grid=(stream.tiles,),
        in_specs=[in_spec(width), in_spec(D_MODEL), _mod_spec(layer, stream), _full((width, D_MODEL)),
                  _full((1, D_MODEL))],
        out_specs=_tok_spec(D_MODEL),
        out_shape=jax.ShapeDtypeStruct((stream.rows, D_MODEL), F32),
        compiler_params=_params(1),
        name="post_odd",
    )(o, xs, mods, w_out, g_post)


def _ffn_kernel(x_ref, xp_ref, xn_ref, mod_ref, g_ref, gp_ref, wu_ref, cw_ref, cb_ref, wd_ref, o_ref, g_scr, act_scr,
                *, stream):
    prev_ok, next_ok = _halo_valid(stream)
    x = x_ref[...]
    h = _pre_norm(x, g_ref, mod_ref, FFN_SLOT)
    h_prev = _pre_norm(xp_ref[...], g_ref, mod_ref, FFN_SLOT)
    h_next = _pre_norm(xn_ref[...], g_ref, mod_ref, FFN_SLOT)
    hb = h.astype(BF16)
    hb_ext = jnp.concatenate([h_prev, h, h_next], axis=0).astype(BF16)
    lo, hi = F_HALO, F_HALO + TM
    for c in range(FFN_HIDDEN // FFN_CHUNK):
        cols = slice(c * FFN_CHUNK, (c + 1) * FFN_CHUNK)
        gate = _dot(hb_ext, wu_ref[:, FFN_HIDDEN + c * FFN_CHUNK:FFN_HIDDEN + (c + 1) * FFN_CHUNK])
        g_scr[0:lo, :] = gate[0:lo] * prev_ok
        g_scr[lo:hi, :] = gate[lo:hi]
        g_scr[hi:, :] = gate[hi:] * next_ok
        conv = (cw_ref[0:1, cols] * g_scr[lo - 1:hi - 1, :] + cw_ref[1:2, cols] * gate[lo:hi]
                + cw_ref[2:3, cols] * g_scr[lo + 1:hi + 1, :] + cb_ref[:, cols])
        val = _dot(hb, wu_ref[:, cols])
        act_scr[:, cols] = (_silu(conv) * val).astype(BF16)
    y = _dot(act_scr[...], wd_ref[...])
    o_ref[...] = x + mod_ref[FFN_GATE:FFN_GATE + 1, :] * (_rms(y) * gp_ref[...])


def _ffn(layer, xs, mods, g_pre, g_post, w_up, conv_w, conv_b, w_down, stream):
    return pl.pallas_call(
        functools.partial(_ffn_kernel, stream=stream),
        grid=(stream.tiles,),
        in_specs=[_tok_spec(D_MODEL), *_halo_specs(D_MODEL, F_HALO, stream), _mod_spec(layer, stream),
                  _full((1, D_MODEL)), _full((1, D_MODEL)), _full((D_MODEL, 2 * FFN_HIDDEN)),
                  _full((SUBLANES, FFN_HIDDEN)), _full((1, FFN_HIDDEN)), _full((FFN_HIDDEN, D_MODEL))],
        out_specs=_tok_spec(D_MODEL),
        out_shape=jax.ShapeDtypeStruct((stream.rows, D_MODEL), F32),
        scratch_shapes=[pltpu.VMEM((TM + 2 * F_HALO, FFN_CHUNK), F32), pltpu.VMEM((TM, FFN_HIDDEN), BF16)],
        compiler_params=_params(1),
        name="conv_ffn",
    )(xs, xs, xs, mods, g_pre, g_post, w_up, conv_w, conv_b, w_down)


def _pad_rows(w, rows):
    return jnp.concatenate([w, jnp.zeros((rows - w.shape[0],) + w.shape[1:], w.dtype)], axis=0)


def _pad_last(w, width):
    return jnp.concatenate([w, jnp.zeros(w.shape[:-1] + (width - w.shape[-1],), w.dtype)], axis=-1)


def _mla_weights(w_in, w_uq, w_ukv, w_out):
    rank = C_Q_RANK + C_KV_RANK
    pe = jnp.concatenate([jnp.zeros((D_MODEL, C_NOPE), F32), w_in[:, rank:],
                          jnp.zeros((D_MODEL, HEAD_PAD - C_NOPE - C_ROPE), F32)], axis=1)
    w_in_p = jnp.concatenate([w_in[:, :rank], pe], axis=1)
    w_q = _pad_last(w_uq.reshape(C_Q_RANK, C_HEADS, C_NOPE + C_ROPE), HEAD_PAD).reshape(C_Q_RANK, -1)
    kv = w_ukv.reshape(C_KV_RANK, C_HEADS, C_NOPE + C_V)
    w_k = _pad_last(kv[..., :C_NOPE], HEAD_PAD).reshape(C_KV_RANK, -1)
    w_vt = kv[..., C_NOPE:].reshape(C_KV_RANK, -1).T
    return tuple(w.astype(BF16) for w in (w_in_p, w_q, w_k, w_vt, w_out))


def kernel(x, c, ctx, c_ctx, ada_w, ada_b, norm_mix_pre, norm_mix_post, norm_ffn_pre, norm_ffn_post, ffn_w_up,
           ffn_conv_w, ffn_conv_b, ffn_w_down, ab_w_in, a_conv_w, a_conv_b, a_ln_g, a_ln_b, b_lambda, b_subln,
           ab_w_out, c_w_in, c_q_norm, c_kv_norm, c_w_uq, c_w_ukv, c_w_out):
    mods = _modulation(c, c_ctx, ada_w, ada_b)
    xs = jnp.concatenate([ctx, x], axis=1).reshape(NT, D_MODEL)
    rope_b = _rope_tables(B_HEAD_DIM, 0, B_HEAD_DIM)
    rope_c = _rope_tables(C_ROPE, C_NOPE, HEAD_PAD)
    row = lambda v: v.reshape(1, -1)
    tokens = lambda t: t.reshape(BATCH, TOK, t.shape[-1])
    flat = lambda t: t.reshape(NT, t.shape[-1])

    for i in range(DEPTH):
        j = i // 2
        stream = UNIFIED if i < DEPTH - 1 else LATENT
        if i % 2 == 0:
            lambda_init = 0.8 - 0.6 * math.exp(-0.3 * i)
            a, q, k, vt = _p1_even(i, xs, mods, row(norm_mix_pre[i]), ab_w_in[j], rope_b)
            ob = _diff_attn(tokens(q), tokens(k), vt, b_lambda[j], row(b_subln[j]), lambda_init)
            xs = _post_even(i, a, flat(ob), xs, mods, ab_w_out[j].astype(BF16),
                            _pad_rows(a_conv_w[j], A_KERNEL + 1), row(a_conv_b[j]), row(a_ln_g[j]),
                            row(a_ln_b[j]), row(norm_mix_post[i]))
        else:
            w_in, w_q, w_k, w_vt, w_o = _mla_weights(c_w_in[j], c_w_uq[j], c_w_ukv[j], c_w_out[j])
            q, k, vt = _p1_odd(i, xs, mods, row(norm_mix_pre[i]), w_in, row(c_q_norm[j]), row(c_kv_norm[j]),
                               w_q, w_k, w_vt, rope_c)
            o = _mla_attn(tokens(q), tokens(k), vt)
            xs = _post_odd(i, flat(o), xs, mods, w_o, row(norm_mix_post[i]), stream)
        xs = _ffn(i, xs, mods, row(norm_ffn_pre[i]), row(norm_ffn_post[i]), ffn_w_up[i].astype(BF16),
                  _pad_rows(ffn_conv_w[i], SUBLANES), row(ffn_conv_b[i]), ffn_w_down[i].astype(BF16), stream)
    return xs.reshape(BATCH, SEQ, D_MODEL)
```

```python
import functools
import math
from typing import NamedTuple

import jax
import jax.numpy as jnp
from jax import lax
from jax.experimental import pallas as pl
from jax.experimental.pallas import tpu as pltpu

D_MODEL = 1024
BATCH = 8
SEQ = 4096
DEPTH = 4
CTX_LEN = 256
GRID_W = 64
ROPE_BASE = 10000.0
RMS_EPS = 1e-6
LN_EPS = 1e-5

A_WIDTH = 512
A_KERNEL = 31
B_HEADS = 4
B_HEAD_DIM = 64
B_V_DIM = 2 * B_HEAD_DIM
B_QK_COLS = B_HEADS * 2 * B_HEAD_DIM
B_WIDTH = B_HEADS * B_V_DIM
AB_IN = 2 * A_WIDTH + 2 * B_QK_COLS + B_WIDTH

C_HEADS = 16
C_NOPE = 64
C_ROPE = 32
C_V = 64
C_Q_RANK = 768
C_KV_RANK = 256

FFN_HIDDEN = 2816
FFN_KERNEL = 3

LANES = 128
SUBLANES = 8
BF16_SUBLANES = 16
MXU_COLS = 256
VMEM_LIMIT_BYTES = 56 * 1024 * 1024

TOK = CTX_LEN + SEQ
NT = BATCH * TOK
TM = 256
TPB = TOK // TM
NTILES = NT // TM
MOD_ROWS = 16
CTX_ROW = BATCH
MIX_SLOT, MIX_GATE, FFN_SLOT, FFN_GATE = 0, 2, 3, 5
HEAD_PAD = LANES
C_HEADS_PER_STEP = 4
B_HEADS_PER_STEP = 4
SCORE_CHUNK = 1024
KV_CHUNK = 256
B_SCORE_LOOKAHEAD = 3
C_SCORE_LOOKAHEAD = 2
FFN_CHUNK = MXU_COLS
A_HALO = 16
F_HALO = SUBLANES

F32 = jnp.float32
BF16 = jnp.bfloat16
LOG2_E = math.log2(math.e)


def _dot(a, b):
    return jnp.dot(a, b, preferred_element_type=F32)


def _dot_nt(a, b):
    return lax.dot_general(a, b, (((1,), (1,)), ((), ())), preferred_element_type=F32)


def _rms(x, eps=RMS_EPS):
    return x * lax.rsqrt(jnp.mean(x * x, axis=-1, keepdims=True) + eps)


def _silu(x):
    return x * jax.nn.sigmoid(x)


def _params(n_axes):
    return pltpu.CompilerParams(dimension_semantics=("parallel",) * n_axes, vmem_limit_bytes=VMEM_LIMIT_BYTES)


def _full(shape):
    return pl.BlockSpec(shape, lambda *_: (0,) * len(shape), pipeline_mode=pl.Buffered(1))


def _tile_in_batch(t):
    return t % TPB


class _Stream(NamedTuple):
    tiles_per_batch: int
    has_ctx: bool

    @property
    def tiles(self):
        return BATCH * self.tiles_per_batch

    @property
    def rows(self):
        return self.tiles * TM


UNIFIED = _Stream(TPB, True)
LATENT = _Stream(TPB - 1, False)
assert DEPTH % 2 == 0, "only the odd-layer tail has a latent-only variant"


def _mod_spec(layer, stream=UNIFIED):
    def index(t):
        row = t // stream.tiles_per_batch
        if stream.has_ctx:
            row = jnp.where(t % stream.tiles_per_batch == 0, CTX_ROW, row)
        return (layer * MOD_ROWS + row, 0, 0)
    return pl.BlockSpec((None, 6, D_MODEL), index)


def _tok_spec(width):
    return pl.BlockSpec((TM, width), lambda t: (t, 0))


def _stream_specs(separate):
    if not separate:
        return [_tok_spec(D_MODEL)]
    latent = lambda t: ((t // TPB) * LATENT.tiles_per_batch + jnp.maximum(_tile_in_batch(t) - 1, 0), 0)
    return [pl.BlockSpec((TM, D_MODEL), lambda t: (t // TPB, 0)), pl.BlockSpec((TM, D_MODEL), latent)]


def _stream_tile(refs):
    if len(refs) == 1:
        return refs[0][...]
    ctx_ref, x_ref = refs
    return jnp.where(_tile_in_batch(pl.program_id(0)) == 0, ctx_ref[...], x_ref[...])


def _latent_of_unified_spec(width):
    return pl.BlockSpec((TM, width), lambda t: ((t // LATENT.tiles_per_batch) * TPB + t % LATENT.tiles_per_batch + 1, 0))


def _halo_specs(width, rows, stream=UNIFIED):
    blocks = TM // rows
    prev = lambda t: (jnp.maximum(t * blocks - 1, 0), 0)
    nxt = lambda t: (jnp.minimum((t + 1) * blocks, stream.rows // rows - 1), 0)
    return [pl.BlockSpec((rows, width), prev), pl.BlockSpec((rows, width), nxt)]


def _halo_valid(stream):
    r = pl.program_id(0) % stream.tiles_per_batch
    first_latent = 1 if stream.has_ctx else 0
    return ((r > first_latent).astype(F32),
            ((r >= first_latent) & (r < stream.tiles_per_batch - 1)).astype(F32))


def _vt_spec(heads, rows):
    return pl.BlockSpec((None, heads, rows, TM), lambda t: (t // TPB, 0, 0, _tile_in_batch(t)))


def _store_vt(vt_ref, vt, heads, dim):
    for h in range(heads):
        vt_ref[h, 0:dim, :] = vt[h * dim:(h + 1) * dim, :].astype(BF16)
        vt_ref[h, dim:dim + BF16_SUBLANES, :] = jnp.ones((BF16_SUBLANES, TM), BF16)


def _rope_spec():
    return pl.BlockSpec((TM, LANES), lambda t: (_tile_in_batch(t), 0))


def _rope(y, c, s_up, s_dn, shift):
    return y * c + pltpu.roll(y, LANES - shift, 1) * s_up + pltpu.roll(y, shift, 1) * s_dn


def _rope_tables(group, lane_offset, period):
    half = group // 4
    lane = jnp.arange(LANES)
    rel = (lane % period) - lane_offset
    active = (rel >= 0) & (rel < group)
    rel = jnp.where(active, rel, 0)
    by_col = rel >= group // 2
    within = rel % (group // 2)
    freq_idx = within % half
    inv_freq = ROPE_BASE ** (-freq_idx.astype(F32) / half)
    t = jnp.arange(SEQ, dtype=jnp.int32)
    pos = jnp.where(by_col[None, :], (t % GRID_W)[:, None], (t // GRID_W)[:, None]).astype(F32)
    ang = pos * inv_freq[None, :]
    cos, sin = jnp.cos(ang), jnp.sin(ang)
    first = within < half
    c = jnp.where(active[None, :], cos, 1.0)
    s_up = jnp.where((active & first)[None, :], -sin, 0.0)
    s_dn = jnp.where((active & ~first)[None, :], sin, 0.0)
    ident = [jnp.ones((CTX_LEN, LANES), F32), jnp.zeros((CTX_LEN, LANES), F32), jnp.zeros((CTX_LEN, LANES), F32)]
    return tuple(jnp.concatenate([i, x.astype(F32)], axis=0) for i, x in zip(ident, (c, s_up, s_dn)))


def _mods_kernel(c_ref, w_ref, b_ref, o_ref):
    s = _silu(c_ref[...])
    o_ref[...] = jnp.dot(s, w_ref[...], preferred_element_type=F32, precision=lax.Precision.HIGHEST) + b_ref[...]


def _modulation(c, c_ctx, ada_w, ada_b):
    rows = jnp.concatenate([c, c_ctx[None, :], jnp.zeros((MOD_ROWS - BATCH - 1, D_MODEL), F32)], axis=0)
    n_blk = 1536
    out = pl.pallas_call(
        _mods_kernel,
        grid=(DEPTH, 6 * D_MODEL // n_blk),
        in_specs=[
            pl.BlockSpec((MOD_ROWS, D_MODEL), lambda i, n: (0, 0)),
            pl.BlockSpec((None, D_MODEL, n_blk), lambda i, n: (i, 0, n)),
            pl.BlockSpec((None, 1, n_blk), lambda i, n: (i, 0, n)),
        ],
        out_specs=pl.BlockSpec((None, MOD_ROWS, n_blk), lambda i, n: (i, 0, n)),
        out_shape=jax.ShapeDtypeStruct((DEPTH, MOD_ROWS, 6 * D_MODEL), F32),
        compiler_params=_params(2),
        name="modulation",
    )(rows, ada_w, ada_b.reshape(DEPTH, 1, 6 * D_MODEL))
    return out.reshape(DEPTH * MOD_ROWS, 6, D_MODEL)


def _pre_norm(x, g_ref, mod_ref, slot):
    return _rms(x) * (g_ref[...] * (1.0 + mod_ref[slot + 1:slot + 2, :])) + mod_ref[slot:slot + 1, :]


def _p1_even_kernel(*refs):
    mod_ref, g_ref, w_ref, wvt_ref, rc_ref, ru_ref, rd_ref, a_ref, q_ref, k_ref, vt_ref = refs[-11:]
    hb = _pre_norm(_stream_tile(refs[:-11]), g_ref, mod_ref, MIX_SLOT).astype(BF16)
    a_ref[...] = _dot(hb, w_ref[:, 0:2 * A_WIDTH]).astype(BF16)
    c, s_up, s_dn = rc_ref[...], ru_ref[...], rd_ref[...]
    o = 2 * A_WIDTH
    for col, out_ref, scale in ((o, q_ref, B_HEAD_DIM ** -0.5 * LOG2_E), (o + B_QK_COLS, k_ref, 1.0)):
        y = _dot(hb, w_ref[:, col:col + B_QK_COLS])
        for j in range(B_QK_COLS // LANES):
            yj = _rope(y[:, j * LANES:(j + 1) * LANES], c, s_up, s_dn, B_HEAD_DIM // 4)
            out_ref[:, j * LANES:(j + 1) * LANES] = (yj * scale).astype(BF16)
    _store_vt(vt_ref, _dot_nt(wvt_ref[...], hb), B_HEADS, B_V_DIM)


def _p1_even(layer, stream_arrays, mods, g, w_in, rope):
    qk_cols = 2 * A_WIDTH + 2 * B_QK_COLS
    vt_rows = B_V_DIM + BF16_SUBLANES
    return pl.pallas_call(
        _p1_even_kernel,
        grid=(NTILES,),
        in_specs=[*_stream_specs(len(stream_arrays) == 2), _mod_spec(layer), _full((1, D_MODEL)),
                  _full((D_MODEL, qk_cols)), _full((B_WIDTH, D_MODEL)), _rope_spec(), _rope_spec(), _rope_spec()],
        out_specs=[_tok_spec(2 * A_WIDTH), _tok_spec(B_QK_COLS), _tok_spec(B_QK_COLS), _vt_spec(B_HEADS, vt_rows)],
        out_shape=[jax.ShapeDtypeStruct((NT, 2 * A_WIDTH), BF16), jax.ShapeDtypeStruct((NT, B_QK_COLS), BF16),
                   jax.ShapeDtypeStruct((NT, B_QK_COLS), BF16),
                   jax.ShapeDtypeStruct((BATCH, B_HEADS, vt_rows, TOK), BF16)],
        compiler_params=_params(1),
        name="p1_even",
    )(*stream_arrays, mods, g, w_in[:, :qk_cols].astype(BF16), w_in[:, qk_cols:].T.astype(BF16), *rope)


def _softmax_step(st, vt, state):
    m_chunk = jnp.max(st, axis=0, keepdims=True)
    m_new = m_chunk if state is None else jnp.maximum(state[0], m_chunk)
    pv = _dot(vt, jnp.exp2((st - m_new).astype(BF16)))
    if state is None:
        return m_new, pv
    return m_new, jnp.exp2(state[0] - m_new) * state[1] + pv


def _attend(streams, lookahead, scores, values, finish):
    ctx_only = [(0, CTX_LEN)]
    everything = ctx_only + [(CTX_LEN + i * SCORE_CHUNK, SCORE_CHUNK) for i in range(SEQ // SCORE_CHUNK)]
    qi = pl.program_id(2)
    for chunks, cond in ((ctx_only, qi == 0), (everything, qi > 0)):
        items = [(start, size, s) for start, size in chunks for s in range(streams)]

        @pl.when(cond)
        def _(items=items):
            state = [None] * streams
            pending = [scores(*item) for item in items[:lookahead]]
            for n, (start, size, s) in enumerate(items):
                if n + lookahead < len(items):
                    pending.append(scores(*items[n + lookahead]))
                st = pending.pop(0)
                step = min(KV_CHUNK, size)
                for j in range(0, size, step):
                    state[s] = _softmax_step(st[j:j + step], values(start + j, step, s), state[s])
            finish([acc for _, acc in state])


def _diff_attn_kernel(lam_ref, sub_ref, q_ref, k_ref, vt_ref, o_ref, *, lambda_init):
    lane = lax.broadcasted_iota(jnp.int32, (TM, LANES), 1)
    zero = jnp.zeros((TM, LANES), BF16)
    qs = []
    for h in range(B_HEADS_PER_STEP):
        q = q_ref[:, h * LANES:(h + 1) * LANES]
        qs += [jnp.where(lane < B_HEAD_DIM, q, zero), jnp.where(lane >= B_HEAD_DIM, q, zero)]

    def scores(start, size, s):
        h = s // 2
        return _dot_nt(k_ref[start:start + size, h * LANES:(h + 1) * LANES], qs[s])

    def values(start, size, s):
        return vt_ref[s // 2, :, start:start + size]

    def finish(accs):
        lv = lam_ref[...]
        lam = (jnp.exp(jnp.sum(lv[0:1] * lv[1:2], axis=-1, keepdims=True))
               - jnp.exp(jnp.sum(lv[2:3] * lv[3:4], axis=-1, keepdims=True)) + lambda_init)
        for h in range(B_HEADS_PER_STEP):
            o1, o2 = (acc[0:B_V_DIM, :] / acc[B_V_DIM:B_V_DIM + 1, :] for acc in accs[2 * h:2 * h + 2])
            o = (o1 - lam * o2).T
            o_ref[:, h * B_V_DIM:(h + 1) * B_V_DIM] = (_rms(o) * (sub_ref[...] * (1.0 - lambda_init))).astype(BF16)

    _attend(2 * B_HEADS_PER_STEP, B_SCORE_LOOKAHEAD, scores, values, finish)


def _diff_attn(q, k, vt, lam_vecs, subln_g, lambda_init):
    rows = B_V_DIM + BF16_SUBLANES
    width = B_HEADS_PER_STEP * LANES
    return pl.pallas_call(
        functools.partial(_diff_attn_kernel, lambda_init=lambda_init),
        grid=(BATCH, B_HEADS // B_HEADS_PER_STEP, TPB),
        in_specs=[pl.BlockSpec((4, B_HEAD_DIM), lambda b, h, i: (0, 0)),
                  pl.BlockSpec((1, B_V_DIM), lambda b, h, i: (0, 0)),
                  pl.BlockSpec((None, TM, width), lambda b, h, i: (b, i, h)),
                  pl.BlockSpec((None, TOK, width), lambda b, h, i: (b, 0, h)),
                  pl.BlockSpec((None, B_HEADS_PER_STEP, rows, TOK), lambda b, h, i: (b, h, 0, 0))],
        out_specs=pl.BlockSpec((None, TM, B_HEADS_PER_STEP * B_V_DIM), lambda b, h, i: (b, i, h)),
        out_shape=jax.ShapeDtypeStruct((BATCH, TOK, B_WIDTH), BF16),
        compiler_params=_params(3),
        name="diff_attn",
    )(lam_vecs, subln_g, q, k, vt)


def _post_even_kernel(a_ref, ap_ref, an_ref, ob_ref, *refs):
    mod_ref, w_ref, cw_ref, cb_ref, lg_ref, lb_ref, gp_ref, o_ref, u_ref = refs[-9:]
    prev_ok, next_ok = _halo_valid(UNIFIED)

    def glu(a):
        a = a.astype(F32)
        return a[:, :A_WIDTH] * jax.nn.sigmoid(a[:, A_WIDTH:])

    ext = TM + 2 * A_HALO
    u_ref[0, 0:A_HALO, :] = glu(ap_ref[...]) * prev_ok
    u_ref[0, A_HALO:A_HALO + TM, :] = glu(a_ref[...])
    u_ref[0, A_HALO + TM:, :] = glu(an_ref[...]) * next_ok
    for s in range(1, SUBLANES):
        u_ref[s, 0:ext - SUBLANES, :] = u_ref[0, s:s + ext - SUBLANES, :]

    rows = 32
    base = A_HALO - A_KERNEL // 2
    outs = []
    for r0 in range(0, TM, rows):
        acc = jnp.broadcast_to(cb_ref[...], (rows, A_WIDTH))
        for tap in range(A_KERNEL):
            s = (base + tap) % SUBLANES
            i0 = r0 + base + tap - s
            acc = acc + cw_ref[tap:tap + 1, :] * u_ref[s, i0:i0 + rows, :]
        mu = jnp.mean(acc, axis=-1, keepdims=True)
        cen = acc - mu
        var = jnp.mean(cen * cen, axis=-1, keepdims=True)
        y = cen * lax.rsqrt(var + LN_EPS) * lg_ref[...] + lb_ref[...]
        outs.append(_silu(y).astype(BF16))
    ya = jnp.concatenate(outs, axis=0)
    y = _dot(ya, w_ref[0:A_WIDTH, :]) + _dot(ob_ref[...], w_ref[A_WIDTH:, :])
    o_ref[...] = _stream_tile(refs[:-9]) + mod_ref[MIX_GATE:MIX_GATE + 1, :] * (_rms(y) * gp_ref[...])


def _post_even(layer, a, ob, stream_arrays, mods, w_out, conv_w, conv_b, ln_g, ln_b, g_post):
    return pl.pallas_call(
        _post_even_kernel,
        grid=(NTILES,),
        in_specs=[_tok_spec(2 * A_WIDTH), *_halo_specs(2 * A_WIDTH, A_HALO),
                  _tok_spec(B_WIDTH), *_stream_specs(len(stream_arrays) == 2), _mod_spec(layer),
                  _full((A_WIDTH + B_WIDTH, D_MODEL)), _full((A_KERNEL + 1, A_WIDTH)), _full((1, A_WIDTH)),
                  _full((1, A_WIDTH)), _full((1, A_WIDTH)), _full((1, D_MODEL))],
        out_specs=_tok_spec(D_MODEL),
        out_shape=jax.ShapeDtypeStruct((NT, D_MODEL), F32),
        scratch_shapes=[pltpu.VMEM((SUBLANES, TM + 2 * A_HALO, A_WIDTH), F32)],
        compiler_params=_params(1),
        name="post_even",
    )(a, a, a, ob, *stream_arrays, mods, w_out, conv_w, conv_b, ln_g, ln_b, g_post)


def _p1_odd_kernel(x_ref, mod_ref, g_ref, w_ref, gq_ref, gkv_ref, wq_ref, wk_ref, wvt_ref,
                   rc_ref, ru_ref, rd_ref, q_ref, k_ref, vt_ref):
    hb = _pre_norm(x_ref[...], g_ref, mod_ref, MIX_SLOT).astype(BF16)
    p = _dot(hb, w_ref[...])
    c, s_up, s_dn = rc_ref[...], ru_ref[...], rd_ref[...]
    shift = C_ROPE // 4
    scale = (C_NOPE + C_ROPE) ** -0.5 * LOG2_E

    cq = (_rms(p[:, :C_Q_RANK]) * gq_ref[...]).astype(BF16)
    q = _dot(cq, wq_ref[...])
    for j in range(C_HEADS):
        sl = slice(j * HEAD_PAD, (j + 1) * HEAD_PAD)
        q_ref[:, sl] = (_rope(q[:, sl], c, s_up, s_dn, shift) * scale).astype(BF16)

    ckv = (_rms(p[:, C_Q_RANK:C_Q_RANK + C_KV_RANK]) * gkv_ref[...]).astype(BF16)
    k_pe = _rope(p[:, C_Q_RANK + C_KV_RANK:], c, s_up, s_dn, shift)
    k = _dot(ckv, wk_ref[...])
    for j in range(C_HEADS):
        sl = slice(j * HEAD_PAD, (j + 1) * HEAD_PAD)
        k_ref[:, sl] = (k[:, sl] + k_pe).astype(BF16)
    _store_vt(vt_ref, _dot_nt(wvt_ref[...], ckv), C_HEADS, C_V)


def _p1_odd(layer, xs, mods, g, w_in, g_q, g_kv, w_q, w_k, w_vt, rope):
    width = C_HEADS * HEAD_PAD
    v_width = C_HEADS * C_V
    vt_rows = C_V + BF16_SUBLANES
    in_cols = C_Q_RANK + C_KV_RANK + HEAD_PAD
    out = jax.ShapeDtypeStruct((NT, width), BF16)
    return pl.pallas_call(
        _p1_odd_kernel,
        grid=(NTILES,),
        in_specs=[_tok_spec(D_MODEL), _mod_spec(layer), _full((1, D_MODEL)),
                  _full((D_MODEL, in_cols)), _full((1, C_Q_RANK)), _full((1, C_KV_RANK)),
                  _full((C_Q_RANK, width)), _full((C_KV_RANK, width)), _full((v_width, C_KV_RANK)),
                  _rope_spec(), _rope_spec(), _rope_spec()],
        out_specs=[_tok_spec(width), _tok_spec(width), _vt_spec(C_HEADS, vt_rows)],
        out_shape=[out, out, jax.ShapeDtypeStruct((BATCH, C_HEADS, vt_rows, TOK), BF16)],
        compiler_params=_params(1),
        name="p1_odd",
    )(xs, mods, g, w_in, g_q, g_kv, w_q, w_k, w_vt, *rope)


def _mla_kernel(q_ref, k_ref, vt_ref, o_ref):
    def scores(start, size, h):
        sl = slice(h * HEAD_PAD, (h + 1) * HEAD_PAD)
        return _dot_nt(k_ref[start:start + size, sl], q_ref[:, sl])

    def values(start, size, h):
        return vt_ref[h, :, start:start + size]

    def finish(accs):
        o_t = [acc[0:C_V, :] / acc[C_V:C_V + 1, :] for acc in accs]
        o_ref[...] = jnp.concatenate(o_t, axis=0).T.astype(BF16)

    _attend(C_HEADS_PER_STEP, C_SCORE_LOOKAHEAD, scores, values, finish)


def _mla_attn(q, k, vt):
    width = C_HEADS_PER_STEP * HEAD_PAD
    rows = C_V + BF16_SUBLANES
    return pl.pallas_call(
        _mla_kernel,
        grid=(BATCH, C_HEADS // C_HEADS_PER_STEP, TPB),
        in_specs=[pl.BlockSpec((None, TM, width), lambda b, h, i: (b, i, h)),
                  pl.BlockSpec((None, TOK, width), lambda b, h, i: (b, 0, h)),
                  pl.BlockSpec((None, C_HEADS_PER_STEP, rows, TOK), lambda b, h, i: (b, h, 0, 0))],
        out_specs=pl.BlockSpec((None, TM, C_HEADS_PER_STEP * C_V), lambda b, h, i: (b, i, h)),
        out_shape=jax.ShapeDtypeStruct((BATCH, TOK, C_HEADS * C_V), BF16),
        compiler_params=_params(3),
        name="mla_attn",
    )(q, k, vt)


def _post_odd_kernel(o_ref, x_ref, mod_ref, w_ref, gp_ref, out_ref):
    y = _dot(o_ref[...], w_ref[...])
    out_ref[...] = x_ref[...] + mod_ref[MIX_GATE:MIX_GATE + 1, :] * (_rms(y) * gp_ref[...])


def _post_odd(layer, o, xs, mods, w_out, g_post, stream):
    width = C_HEADS * C_V
    in_spec = _tok_spec if stream.has_ctx else _latent_of_unified_spec
    return pl.pallas_call(
        _post_odd_kernel,
        grid=(stream.tiles,),
        in_specs=[in_spec(width), in_spec(D_MODEL), _mod_spec(layer, stream), _full((width, D_MODEL)),
                  _full((1, D_MODEL))],
        out_specs=_tok_spec(D_MODEL),
        out_shape=jax.ShapeDtypeStruct((stream.rows, D_MODEL), F32),
        compiler_params=_params(1),
        name="post_odd",
    )(o, xs, mods, w_out, g_post)


def _ffn_kernel(x_ref, xp_ref, xn_ref, mod_ref, g_ref, gp_ref, wu_ref, cw_ref, cb_ref, wd_ref, o_ref, g_scr, act_scr,
                *, stream):
    prev_ok, next_ok = _halo_valid(stream)
    x = x_ref[...]
    h = _pre_norm(x, g_ref, mod_ref, FFN_SLOT)
    h_prev = _pre_norm(xp_ref[...], g_ref, mod_ref, FFN_SLOT)
    h_next = _pre_norm(xn_ref[...], g_ref, mod_ref, FFN_SLOT)
    hb = h.astype(BF16)
    hb_ext = jnp.concatenate([h_prev, h, h_next], axis=0).astype(BF16)
    lo, hi = F_HALO, F_HALO + TM
    for c in range(FFN_HIDDEN // FFN_CHUNK):
        cols = slice(c * FFN_CHUNK, (c + 1) * FFN_CHUNK)
        gate = _dot(hb_ext, wu_ref[:, FFN_HIDDEN + c * FFN_CHUNK:FFN_HIDDEN + (c + 1) * FFN_CHUNK])
        g_scr[0:lo, :] = gate[0:lo] * prev_ok
        g_scr[lo:hi, :] = gate[lo:hi]
        g_scr[hi:, :] = gate[hi:] * next_ok
        conv = (cw_ref[0:1, cols] * g_scr[lo - 1:hi - 1, :] + cw_ref[1:2, cols] * gate[lo:hi]
                + cw_ref[2:3, cols] * g_scr[lo + 1:hi + 1, :] + cb_ref[:, cols])
        val = _dot(hb, wu_ref[:, cols])
        act_scr[:, cols] = (_silu(conv) * val).astype(BF16)
    y = _dot(act_scr[...], wd_ref[...])
    o_ref[...] = x + mod_ref[FFN_GATE:FFN_GATE + 1, :] * (_rms(y) * gp_ref[...])


def _ffn(layer, xs, mods, g_pre, g_post, w_up, conv_w, conv_b, w_down, stream):
    return pl.pallas_call(
        functools.partial(_ffn_kernel, stream=stream),
        grid=(stream.tiles,),
        in_specs=[_tok_spec(D_MODEL), *_halo_specs(D_MODEL, F_HALO, stream), _mod_spec(layer, stream),
                  _full((1, D_MODEL)), _full((1, D_MODEL)), _full((D_MODEL, 2 * FFN_HIDDEN)),
                  _full((SUBLANES, FFN_HIDDEN)), _full((1, FFN_HIDDEN)), _full((FFN_HIDDEN, D_MODEL))],
        out_specs=_tok_spec(D_MODEL),
        out_shape=jax.ShapeDtypeStruct((stream.rows, D_MODEL), F32),
        scratch_shapes=[pltpu.VMEM((TM + 2 * F_HALO, FFN_CHUNK), F32), pltpu.VMEM((TM, FFN_HIDDEN), BF16)],
        compiler_params=_params(1),
        name="conv_ffn",
    )(xs, xs, xs, mods, g_pre, g_post, w_up, conv_w, conv_b, w_down)


def _pad_rows(w, rows):
    return jnp.concatenate([w, jnp.zeros((rows - w.shape[0],) + w.shape[1:], w.dtype)], axis=0)


def _pad_last(w, width):
    return jnp.concatenate([w, jnp.zeros(w.shape[:-1] + (width - w.shape[-1],), w.dtype)], axis=-1)


def _mla_weights(w_in, w_uq, w_ukv, w_out):
    rank = C_Q_RANK + C_KV_RANK
    pe = jnp.concatenate([jnp.zeros((D_MODEL, C_NOPE), F32), w_in[:, rank:],
                          jnp.zeros((D_MODEL, HEAD_PAD - C_NOPE - C_ROPE), F32)], axis=1)
    w_in_p = jnp.concatenate([w_in[:, :rank], pe], axis=1)
    w_q = _pad_last(w_uq.reshape(C_Q_RANK, C_HEADS, C_NOPE + C_ROPE), HEAD_PAD).reshape(C_Q_RANK, -1)
    kv = w_ukv.reshape(C_KV_RANK, C_HEADS, C_NOPE + C_V)
    w_k = _pad_last(kv[..., :C_NOPE], HEAD_PAD).reshape(C_KV_RANK, -1)
    w_vt = kv[..., C_NOPE:].reshape(C_KV_RANK, -1).T
    return tuple(w.astype(BF16) for w in (w_in_p, w_q, w_k, w_vt, w_out))


def kernel(x, c, ctx, c_ctx, ada_w, ada_b, norm_mix_pre, norm_mix_post, norm_ffn_pre, norm_ffn_post, ffn_w_up,
           ffn_conv_w, ffn_conv_b, ffn_w_down, ab_w_in, a_conv_w, a_conv_b, a_ln_g, a_ln_b, b_lambda, b_subln,
           ab_w_out, c_w_in, c_q_norm, c_kv_norm, c_w_uq, c_w_ukv, c_w_out):
    mods = _modulation(c, c_ctx, ada_w, ada_b)
    stream_arrays = (ctx.reshape(BATCH * CTX_LEN, D_MODEL), x.reshape(BATCH * SEQ, D_MODEL))
    rope_b = _rope_tables(B_HEAD_DIM, 0, B_HEAD_DIM)
    rope_c = _rope_tables(C_ROPE, C_NOPE, HEAD_PAD)
    row = lambda v: v.reshape(1, -1)
    tokens = lambda t: t.reshape(BATCH, TOK, t.shape[-1])
    flat = lambda t: t.reshape(NT, t.shape[-1])

    for i in range(DEPTH):
        j = i // 2
        stream = UNIFIED if i < DEPTH - 1 else LATENT
        if i % 2 == 0:
            lambda_init = 0.8 - 0.6 * math.exp(-0.3 * i)
            a, q, k, vt = _p1_even(i, stream_arrays, mods, row(norm_mix_pre[i]), ab_w_in[j], rope_b)
            ob = _diff_attn(tokens(q), tokens(k), vt, b_lambda[j], row(b_subln[j]), lambda_init)
            xs = _post_even(i, a, flat(ob), stream_arrays, mods, ab_w_out[j].astype(BF16),
                            _pad_rows(a_conv_w[j], A_KERNEL + 1), row(a_conv_b[j]), row(a_ln_g[j]),
                            row(a_ln_b[j]), row(norm_mix_post[i]))
        else:
            w_in, w_q, w_k, w_vt, w_o = _mla_weights(c_w_in[j], c_w_uq[j], c_w_ukv[j], c_w_out[j])
            q, k, vt = _p1_odd(i, xs, mods, row(norm_mix_pre[i]), w_in, row(c_q_norm[j]), row(c_kv_norm[j]),
                               w_q, w_k, w_vt, rope_c)
            o = _mla_attn(tokens(q), tokens(k), vt)
            xs = _post_odd(i, flat(o), xs, mods, w_o, row(norm_mix_post[i]), stream)
        xs = _ffn(i, xs, mods, row(norm_ffn_pre[i]), row(norm_ffn_post[i]), ffn_w_up[i].astype(BF16),
                  _pad_rows(ffn_conv_w[i], SUBLANES), row(ffn_conv_b[i]), ffn_w_down[i].astype(BF16), stream)
        stream_arrays = (xs,)
    return xs.reshape(BATCH, SEQ, D_MODEL)
```

```python
import functools
import math
from typing import NamedTuple

import jax
import jax.numpy as jnp
from jax import lax
from jax.experimental import pallas as pl
from jax.experimental.pallas import tpu as pltpu

D_MODEL = 1024
BATCH = 8
SEQ = 4096
DEPTH = 4
CTX_LEN = 256
GRID_W = 64
ROPE_BASE = 10000.0
RMS_EPS = 1e-6
LN_EPS = 1e-5

A_WIDTH = 512
A_KERNEL = 31
B_HEADS = 4
B_HEAD_DIM = 64
B_V_DIM = 2 * B_HEAD_DIM
B_QK_COLS = B_HEADS * 2 * B_HEAD_DIM
B_WIDTH = B_HEADS * B_V_DIM
AB_IN = 2 * A_WIDTH + 2 * B_QK_COLS + B_WIDTH

C_HEADS = 16
C_NOPE = 64
C_ROPE = 32
C_V = 64
C_Q_RANK = 768
C_KV_RANK = 256

FFN_HIDDEN = 2816
FFN_KERNEL = 3

LANES = 128
SUBLANES = 8
BF16_SUBLANES = 16
MXU_COLS = 256
VMEM_LIMIT_BYTES = 56 * 1024 * 1024

TOK = CTX_LEN + SEQ
NT = BATCH * TOK
TM = 256
TPB = TOK // TM
NTILES = NT // TM
MOD_ROWS = 16
CTX_ROW = BATCH
MIX_SLOT, MIX_GATE, FFN_SLOT, FFN_GATE = 0, 2, 3, 5
HEAD_PAD = LANES
C_HEADS_PER_STEP = 4
B_HEADS_PER_STEP = 4
SCORE_CHUNK = 1024
KV_CHUNK = 256
B_SCORE_LOOKAHEAD = 3
C_SCORE_LOOKAHEAD = 2
FFN_CHUNK = MXU_COLS
A_HALO = 16
F_HALO = SUBLANES

F32 = jnp.float32
BF16 = jnp.bfloat16
LOG2_E = math.log2(math.e)


def _dot(a, b):
    return jnp.dot(a, b, preferred_element_type=F32)


def _dot_nt(a, b):
    return lax.dot_general(a, b, (((1,), (1,)), ((), ())), preferred_element_type=F32)


def _rms(x, eps=RMS_EPS):
    return x * lax.rsqrt(jnp.mean(x * x, axis=-1, keepdims=True) + eps)


def _silu(x):
    return x * jax.nn.sigmoid(x)


def _params(n_axes):
    return pltpu.CompilerParams(dimension_semantics=("parallel",) * n_axes, vmem_limit_bytes=VMEM_LIMIT_BYTES)


def _full(shape):
    return pl.BlockSpec(shape, lambda *_: (0,) * len(shape), pipeline_mode=pl.Buffered(1))


def _tile_in_batch(t):
    return t % TPB


class _Stream(NamedTuple):
    tiles_per_batch: int
    has_ctx: bool

    @property
    def tiles(self):
        return BATCH * self.tiles_per_batch

    @property
    def rows(self):
        return self.tiles * TM


UNIFIED = _Stream(TPB, True)
LATENT = _Stream(TPB - 1, False)
assert DEPTH % 2 == 0, "only the odd-layer tail has a latent-only variant"


def _mod_spec(layer, stream=UNIFIED):
    def index(t):
        row = t // stream.tiles_per_batch
        if stream.has_ctx:
            row = jnp.where(t % stream.tiles_per_batch == 0, CTX_ROW, row)
        return (layer * MOD_ROWS + row, 0, 0)
    return pl.BlockSpec((None, 6, D_MODEL), index)


def _tok_spec(width):
    return pl.BlockSpec((TM, width), lambda t: (t, 0))


def _stream_specs(separate):
    if not separate:
        return [_tok_spec(D_MODEL)]
    latent = lambda t: ((t // TPB) * LATENT.tiles_per_batch + jnp.maximum(_tile_in_batch(t) - 1, 0), 0)
    return [pl.BlockSpec((TM, D_MODEL), lambda t: (t // TPB, 0)), pl.BlockSpec((TM, D_MODEL), latent)]


def _stream_tile(refs):
    if len(refs) == 1:
        return refs[0][...]
    ctx_ref, x_ref = refs
    return jnp.where(_tile_in_batch(pl.program_id(0)) == 0, ctx_ref[...], x_ref[...])


def _latent_of_unified_spec(width):
    return pl.BlockSpec((TM, width), lambda t: ((t // LATENT.tiles_per_batch) * TPB + t % LATENT.tiles_per_batch + 1, 0))


def _halo_specs(width, rows, stream=UNIFIED):
    blocks = TM // rows
    prev = lambda t: (jnp.maximum(t * blocks - 1, 0), 0)
    nxt = lambda t: (jnp.minimum((t + 1) * blocks, stream.rows // rows - 1), 0)
    return [pl.BlockSpec((rows, width), prev), pl.BlockSpec((rows, width), nxt)]


def _halo_valid(stream):
    r = pl.program_id(0) % stream.tiles_per_batch
    first_latent = 1 if stream.has_ctx else 0
    return ((r > first_latent).astype(F32),
            ((r >= first_latent) & (r < stream.tiles_per_batch - 1)).astype(F32))


def _vt_spec(heads, rows):
    return pl.BlockSpec((None, heads, rows, TM), lambda t: (t // TPB, 0, 0, _tile_in_batch(t)))


def _store_vt(vt_ref, vt, heads, dim):
    for h in range(heads):
        vt_ref[h, 0:dim, :] = vt[h * dim:(h + 1) * dim, :].astype(BF16)
        vt_ref[h, dim:dim + BF16_SUBLANES, :] = jnp.ones((BF16_SUBLANES, TM), BF16)


def _rope_spec():
    return pl.BlockSpec((TM, LANES), lambda t: (_tile_in_batch(t), 0))


def _rope(y, c, s_up, s_dn, shift):
    return y * c + pltpu.roll(y, LANES - shift, 1) * s_up + pltpu.roll(y, shift, 1) * s_dn


def _rope_tables(group, lane_offset, period):
    half = group // 4
    lane = jnp.arange(LANES)
    rel = (lane % period) - lane_offset
    active = (rel >= 0) & (rel < group)
    rel = jnp.where(active, rel, 0)
    by_col = rel >= group // 2
    within = rel % (group // 2)
    freq_idx = within % half
    inv_freq = ROPE_BASE ** (-freq_idx.astype(F32) / half)
    t = jnp.arange(SEQ, dtype=jnp.int32)
    pos = jnp.where(by_col[None, :], (t % GRID_W)[:, None], (t // GRID_W)[:, None]).astype(F32)
    ang = pos * inv_freq[None, :]
    cos, sin = jnp.cos(ang), jnp.sin(ang)
    first = within < half
    c = jnp.where(active[None, :], cos, 1.0)
    s_up = jnp.where((active & first)[None, :], -sin, 0.0)
    s_dn = jnp.where((active & ~first)[None, :], sin, 0.0)
    ident = [jnp.ones((CTX_LEN, LANES), F32), jnp.zeros((CTX_LEN, LANES), F32), jnp.zeros((CTX_LEN, LANES), F32)]
    return tuple(jnp.concatenate([i, x.astype(F32)], axis=0) for i, x in zip(ident, (c, s_up, s_dn)))


def _mods_kernel(c_ref, w_ref, b_ref, o_ref):
    s = _silu(c_ref[...])
    o_ref[...] = jnp.dot(s, w_ref[...], preferred_element_type=F32, precision=lax.Precision.HIGHEST) + b_ref[...]


def _modulation(c, c_ctx, ada_w, ada_b):
    rows = jnp.concatenate([c, c_ctx[None, :], jnp.zeros((MOD_ROWS - BATCH - 1, D_MODEL), F32)], axis=0)
    n_blk = 1536
    out = pl.pallas_call(
        _mods_kernel,
        grid=(DEPTH, 6 * D_MODEL // n_blk),
        in_specs=[
            pl.BlockSpec((MOD_ROWS, D_MODEL), lambda i, n: (0, 0)),
            pl.BlockSpec((None, D_MODEL, n_blk), lambda i, n: (i, 0, n)),
            pl.BlockSpec((None, 1, n_blk), lambda i, n: (i, 0, n)),
        ],
        out_specs=pl.BlockSpec((None, MOD_ROWS, n_blk), lambda i, n: (i, 0, n)),
        out_shape=jax.ShapeDtypeStruct((DEPTH, MOD_ROWS, 6 * D_MODEL), F32),
        compiler_params=_params(2),
        name="modulation",
    )(rows, ada_w, ada_b.reshape(DEPTH, 1, 6 * D_MODEL))
    return out.reshape(DEPTH * MOD_ROWS, 6, D_MODEL)


def _pre_norm(x, g_ref, mod_ref, slot):
    return _rms(x) * (g_ref[...] * (1.0 + mod_ref[slot + 1:slot + 2, :])) + mod_ref[slot:slot + 1, :]


def _p1_even_kernel(*refs):
    mod_ref, g_ref, w_ref, wvt_ref, rc_ref, ru_ref, rd_ref, a_ref, q_ref, k_ref, vt_ref = refs[-11:]
    hb = _pre_norm(_stream_tile(refs[:-11]), g_ref, mod_ref, MIX_SLOT).astype(BF16)
    a_ref[...] = _dot(hb, w_ref[:, 0:2 * A_WIDTH]).astype(BF16)
    c, s_up, s_dn = rc_ref[...], ru_ref[...], rd_ref[...]
    o = 2 * A_WIDTH
    for col, out_ref, scale in ((o, q_ref, B_HEAD_DIM ** -0.5 * LOG2_E), (o + B_QK_COLS, k_ref, 1.0)):
        y = _dot(hb, w_ref[:, col:col + B_QK_COLS])
        for j in range(B_QK_COLS // LANES):
            yj = _rope(y[:, j * LANES:(j + 1) * LANES], c, s_up, s_dn, B_HEAD_DIM // 4)
            out_ref[:, j * LANES:(j + 1) * LANES] = (yj * scale).astype(BF16)
    _store_vt(vt_ref, _dot_nt(wvt_ref[...], hb), B_HEADS, B_V_DIM)


def _p1_even(layer, stream_arrays, mods, g, w_in, rope):
    qk_cols = 2 * A_WIDTH + 2 * B_QK_COLS
    vt_rows = B_V_DIM + BF16_SUBLANES
    return pl.pallas_call(
        _p1_even_kernel,
        grid=(NTILES,),
        in_specs=[*_stream_specs(len(stream_arrays) == 2), _mod_spec(layer), _full((1, D_MODEL)),
                  _full((D_MODEL, qk_cols)), _full((B_WIDTH, D_MODEL)), _rope_spec(), _rope_spec(), _rope_spec()],
        out_specs=[_tok_spec(2 * A_WIDTH), _tok_spec(B_QK_COLS), _tok_spec(B_QK_COLS), _vt_spec(B_HEADS, vt_rows)],
        out_shape=[jax.ShapeDtypeStruct((NT, 2 * A_WIDTH), BF16), jax.ShapeDtypeStruct((NT, B_QK_COLS), BF16),
                   jax.ShapeDtypeStruct((NT, B_QK_COLS), BF16),
                   jax.ShapeDtypeStruct((BATCH, B_HEADS, vt_rows, TOK), BF16)],
        compiler_params=_params(1),
        name="p1_even",
    )(*stream_arrays, mods, g, w_in[:, :qk_cols].astype(BF16), w_in[:, qk_cols:].T.astype(BF16), *rope)


def _softmax_step(st, vt, state):
    m_chunk = jnp.max(st, axis=0, keepdims=True)
    m_new = m_chunk if state is None else jnp.maximum(state[0], m_chunk)
    pv = _dot(vt, jnp.exp2((st - m_new).astype(BF16)))
    if state is None:
        return m_new, pv
    return m_new, jnp.exp2(state[0] - m_new) * state[1] + pv


def _attend(streams, lookahead, join_context, scores, values, finish):
    ctx_only = [(0, CTX_LEN)]
    latent = [(CTX_LEN + i * SCORE_CHUNK, SCORE_CHUNK) for i in range(SEQ // SCORE_CHUNK)]
    everything = [(0, CTX_LEN + SCORE_CHUNK)] + latent[1:] if join_context else ctx_only + latent
    qi = pl.program_id(2)
    for chunks, cond in ((ctx_only, qi == 0), (everything, qi > 0)):
        items = [(start, size, s) for start, size in chunks for s in range(streams)]

        @pl.when(cond)
        def _(items=items):
            state = [None] * streams
            pending = [scores(*item) for item in items[:lookahead]]
            for n, (start, size, s) in enumerate(items):
                if n + lookahead < len(items):
                    pending.append(scores(*items[n + lookahead]))
                st = pending.pop(0)
                step = min(KV_CHUNK, size)
                for j in range(0, size, step):
                    state[s] = _softmax_step(st[j:j + step], values(start + j, step, s), state[s])
            finish([acc for _, acc in state])


def _diff_attn_kernel(lam_ref, sub_ref, q_ref, k_ref, vt_ref, o_ref, *, lambda_init):
    lane = lax.broadcasted_iota(jnp.int32, (TM, LANES), 1)
    zero = jnp.zeros((TM, LANES), BF16)
    qs = []
    for h in range(B_HEADS_PER_STEP):
        q = q_ref[:, h * LANES:(h + 1) * LANES]
        qs += [jnp.where(lane < B_HEAD_DIM, q, zero), jnp.where(lane >= B_HEAD_DIM, q, zero)]

    def scores(start, size, s):
        h = s // 2
        return _dot_nt(k_ref[start:start + size, h * LANES:(h + 1) * LANES], qs[s])

    def values(start, size, s):
        return vt_ref[s // 2, :, start:start + size]

    def finish(accs):
        lv = lam_ref[...]
        lam = (jnp.exp(jnp.sum(lv[0:1] * lv[1:2], axis=-1, keepdims=True))
               - jnp.exp(jnp.sum(lv[2:3] * lv[3:4], axis=-1, keepdims=True)) + lambda_init)
        for h in range(B_HEADS_PER_STEP):
            o1, o2 = (acc[0:B_V_DIM, :] / acc[B_V_DIM:B_V_DIM + 1, :] for acc in accs[2 * h:2 * h + 2])
            o = (o1 - lam * o2).T
            o_ref[:, h * B_V_DIM:(h + 1) * B_V_DIM] = (_rms(o) * (sub_ref[...] * (1.0 - lambda_init))).astype(BF16)

    _attend(2 * B_HEADS_PER_STEP, B_SCORE_LOOKAHEAD, False, scores, values, finish)


def _diff_attn(q, k, vt, lam_vecs, subln_g, lambda_init):
    rows = B_V_DIM + BF16_SUBLANES
    width = B_HEADS_PER_STEP * LANES
    return pl.pallas_call(
        functools.partial(_diff_attn_kernel, lambda_init=lambda_init),
        grid=(BATCH, B_HEADS // B_HEADS_PER_STEP, TPB),
        in_specs=[pl.BlockSpec((4, B_HEAD_DIM), lambda b, h, i: (0, 0)),
                  pl.BlockSpec((1, B_V_DIM), lambda b, h, i: (0, 0)),
                  pl.BlockSpec((None, TM, width), lambda b, h, i: (b, i, h)),
                  pl.BlockSpec((None, TOK, width), lambda b, h, i: (b, 0, h)),
                  pl.BlockSpec((None, B_HEADS_PER_STEP, rows, TOK), lambda b, h, i: (b, h, 0, 0))],
        out_specs=pl.BlockSpec((None, TM, B_HEADS_PER_STEP * B_V_DIM), lambda b, h, i: (b, i, h)),
        out_shape=jax.ShapeDtypeStruct((BATCH, TOK, B_WIDTH), BF16),
        compiler_params=_params(3),
        name="diff_attn",
    )(lam_vecs, subln_g, q, k, vt)


def _post_even_kernel(a_ref, ap_ref, an_ref, ob_ref, *refs):
    mod_ref, w_ref, cw_ref, cb_ref, lg_ref, lb_ref, gp_ref, o_ref, u_ref = refs[-9:]
    prev_ok, next_ok = _halo_valid(UNIFIED)

    def glu(a):
        a = a.astype(F32)
        return a[:, :A_WIDTH] * jax.nn.sigmoid(a[:, A_WIDTH:])

    ext = TM + 2 * A_HALO
    u_ref[0, 0:A_HALO, :] = glu(ap_ref[...]) * prev_ok
    u_ref[0, A_HALO:A_HALO + TM, :] = glu(a_ref[...])
    u_ref[0, A_HALO + TM:, :] = glu(an_ref[...]) * next_ok
    for s in range(1, SUBLANES):
        u_ref[s, 0:ext - SUBLANES, :] = u_ref[0, s:s + ext - SUBLANES, :]

    rows = 32
    base = A_HALO - A_KERNEL // 2
    outs = []
    for r0 in range(0, TM, rows):
        acc = jnp.broadcast_to(cb_ref[...], (rows, A_WIDTH))
        for tap in range(A_KERNEL):
            s = (base + tap) % SUBLANES
            i0 = r0 + base + tap - s
            acc = acc + cw_ref[tap:tap + 1, :] * u_ref[s, i0:i0 + rows, :]
        mu = jnp.mean(acc, axis=-1, keepdims=True)
        cen = acc - mu
        var = jnp.mean(cen * cen, axis=-1, keepdims=True)
        y = cen * lax.rsqrt(var + LN_EPS) * lg_ref[...] + lb_ref[...]
        outs.append(_silu(y).astype(BF16))
    ya = jnp.concatenate(outs, axis=0)
    y = _dot(ya, w_ref[0:A_WIDTH, :]) + _dot(ob_ref[...], w_ref[A_WIDTH:, :])
    o_ref[...] = _stream_tile(refs[:-9]) + mod_ref[MIX_GATE:MIX_GATE + 1, :] * (_rms(y) * gp_ref[...])


def _post_even(layer, a, ob, stream_arrays, mods, w_out, conv_w, conv_b, ln_g, ln_b, g_post):
    return pl.pallas_call(
        _post_even_kernel,
        grid=(NTILES,),
        in_specs=[_tok_spec(2 * A_WIDTH), *_halo_specs(2 * A_WIDTH, A_HALO),
                  _tok_spec(B_WIDTH), *_stream_specs(len(stream_arrays) == 2), _mod_spec(layer),
                  _full((A_WIDTH + B_WIDTH, D_MODEL)), _full((A_KERNEL + 1, A_WIDTH)), _full((1, A_WIDTH)),
                  _full((1, A_WIDTH)), _full((1, A_WIDTH)), _full((1, D_MODEL))],
        out_specs=_tok_spec(D_MODEL),
        out_shape=jax.ShapeDtypeStruct((NT, D_MODEL), F32),
        scratch_shapes=[pltpu.VMEM((SUBLANES, TM + 2 * A_HALO, A_WIDTH), F32)],
        compiler_params=_params(1),
        name="post_even",
    )(a, a, a, ob, *stream_arrays, mods, w_out, conv_w, conv_b, ln_g, ln_b, g_post)


def _p1_odd_kernel(x_ref, mod_ref, g_ref, w_ref, gq_ref, gkv_ref, wq_ref, wk_ref, wvt_ref,
                   rc_ref, ru_ref, rd_ref, q_ref, k_ref, vt_ref):
    hb = _pre_norm(x_ref[...], g_ref, mod_ref, MIX_SLOT).astype(BF16)
    p = _dot(hb, w_ref[...])
    c, s_up, s_dn = rc_ref[...], ru_ref[...], rd_ref[...]
    shift = C_ROPE // 4
    scale = (C_NOPE + C_ROPE) ** -0.5 * LOG2_E

    cq = (_rms(p[:, :C_Q_RANK]) * gq_ref[...]).astype(BF16)
    q = _dot(cq, wq_ref[...])
    for j in range(C_HEADS):
        sl = slice(j * HEAD_PAD, (j + 1) * HEAD_PAD)
        q_ref[:, sl] = (_rope(q[:, sl], c, s_up, s_dn, shift) * scale).astype(BF16)

    ckv = (_rms(p[:, C_Q_RANK:C_Q_RANK + C_KV_RANK]) * gkv_ref[...]).astype(BF16)
    k_pe = _rope(p[:, C_Q_RANK + C_KV_RANK:], c, s_up, s_dn, shift)
    k = _dot(ckv, wk_ref[...])
    for j in range(C_HEADS):
        sl = slice(j * HEAD_PAD, (j + 1) * HEAD_PAD)
        k_ref[:, sl] = (k[:, sl] + k_pe).astype(BF16)
    _store_vt(vt_ref, _dot_nt(wvt_ref[...], ckv), C_HEADS, C_V)


def _p1_odd(layer, xs, mods, g, w_in, g_q, g_kv, w_q, w_k, w_vt, rope):
    width = C_HEADS * HEAD_PAD
    v_width = C_HEADS * C_V
    vt_rows = C_V + BF16_SUBLANES
    in_cols = C_Q_RANK + C_KV_RANK + HEAD_PAD
    out = jax.ShapeDtypeStruct((NT, width), BF16)
    return pl.pallas_call(
        _p1_odd_kernel,
        grid=(NTILES,),
        in_specs=[_tok_spec(D_MODEL), _mod_spec(layer), _full((1, D_MODEL)),
                  _full((D_MODEL, in_cols)), _full((1, C_Q_RANK)), _full((1, C_KV_RANK)),
                  _full((C_Q_RANK, width)), _full((C_KV_RANK, width)), _full((v_width, C_KV_RANK)),
                  _rope_spec(), _rope_spec(), _rope_spec()],
        out_specs=[_tok_spec(width), _tok_spec(width), _vt_spec(C_HEADS, vt_rows)],
        out_shape=[out, out, jax.ShapeDtypeStruct((BATCH, C_HEADS, vt_rows, TOK), BF16)],
        compiler_params=_params(1),
        name="p1_odd",
    )(xs, mods, g, w_in, g_q, g_kv, w_q, w_k, w_vt, *rope)


def _mla_kernel(q_ref, k_ref, vt_ref, o_ref):
    def scores(start, size, h):
        sl = slice(h * HEAD_PAD, (h + 1) * HEAD_PAD)
        return _dot_nt(k_ref[start:start + size, sl], q_ref[:, sl])

    def values(start, size, h):
        return vt_ref[h, :, start:start + size]

    def finish(accs):
        o_t = [acc[0:C_V, :] / acc[C_V:C_V + 1, :] for acc in accs]
        o_ref[...] = jnp.concatenate(o_t, axis=0).T.astype(BF16)

    _attend(C_HEADS_PER_STEP, C_SCORE_LOOKAHEAD, True, scores, values, finish)


def _mla_attn(q, k, vt):
    width = C_HEADS_PER_STEP * HEAD_PAD
    rows = C_V + BF16_SUBLANES
    return pl.pallas_call(
        _mla_kernel,
        grid=(BATCH, C_HEADS // C_HEADS_PER_STEP, TPB),
        in_specs=[pl.BlockSpec((None, TM, width), lambda b, h, i: (b, i, h)),
                  pl.BlockSpec((None, TOK, width), lambda b, h, i: (b, 0, h)),
                  pl.BlockSpec((None, C_HEADS_PER_STEP, rows, TOK), lambda b, h, i: (b, h, 0, 0))],
        out_specs=pl.BlockSpec((None, TM, C_HEADS_PER_STEP * C_V), lambda b, h, i: (b, i, h)),
        out_shape=jax.ShapeDtypeStruct((BATCH, TOK, C_HEADS * C_V), BF16),
        compiler_params=_params(3),
        name="mla_attn",
    )(q, k, vt)


def _post_odd_kernel(o_ref, x_ref, mod_ref, w_ref, gp_ref, out_ref):
    y = _dot(o_ref[...], w_ref[...])
    out_ref[...] = x_ref[...] + mod_ref[MIX_GATE:MIX_GATE + 1, :] * (_rms(y) * gp_ref[...])


def _post_odd(layer, o, xs, mods, w_out, g_post, stream):
    width = C_HEADS * C_V
    in_spec = _tok_spec if stream.has_ctx else _latent_of_unified_spec
    return pl.pallas_call(
        _post_odd_kernel,
        grid=(stream.tiles,),
        in_specs=[in_spec(width), in_spec(D_MODEL), _mod_spec(layer, stream), _full((width, D_MODEL)),
                  _full((1, D_MODEL))],
        out_specs=_tok_spec(D_MODEL),
        out_shape=jax.ShapeDtypeStruct((stream.rows, D_MODEL), F32),
        compiler_params=_params(1),
        name="post_odd",
    )(o, xs, mods, w_out, g_post)


def _ffn_kernel(x_ref, xp_ref, xn_ref, mod_ref, g_ref, gp_ref, wu_ref, cw_ref, cb_ref, wd_ref, o_ref, g_scr, act_scr,
                *, stream):
    prev_ok, next_ok = _halo_valid(stream)
    x = x_ref[...]
    h = _pre_norm(x, g_ref, mod_ref, FFN_SLOT)
    h_prev = _pre_norm(xp_ref[...], g_ref, mod_ref, FFN_SLOT)
    h_next = _pre_norm(xn_ref[...], g_ref, mod_ref, FFN_SLOT)
    hb = h.astype(BF16)
    hb_ext = jnp.concatenate([h_prev, h, h_next], axis=0).astype(BF16)
    lo, hi = F_HALO, F_HALO + TM
    for c in range(FFN_HIDDEN // FFN_CHUNK):
        cols = slice(c * FFN_CHUNK, (c + 1) * FFN_CHUNK)
        gate = _dot(hb_ext, wu_ref[:, FFN_HIDDEN + c * FFN_CHUNK:FFN_HIDDEN + (c + 1) * FFN_CHUNK])
        g_scr[0:lo, :] = gate[0:lo] * prev_ok
        g_scr[lo:hi, :] = gate[lo:hi]
        g_scr[hi:, :] = gate[hi:] * next_ok
        conv = (cw_ref[0:1, cols] * g_scr[lo - 1:hi - 1, :] + cw_ref[1:2, cols] * gate[lo:hi]
                + cw_ref[2:3, cols] * g_scr[lo + 1:hi + 1, :] + cb_ref[:, cols])
        val = _dot(hb, wu_ref[:, cols])
        act_scr[:, cols] = (_silu(conv) * val).astype(BF16)
    y = _dot(act_scr[...], wd_ref[...])
    o_ref[...] = x + mod_ref[FFN_GATE:FFN_GATE + 1, :] * (_rms(y) * gp_ref[...])


def _ffn(layer, xs, mods, g_pre, g_post, w_up, conv_w, conv_b, w_down, stream):
    return pl.pallas_call(
        functools.partial(_ffn_kernel, stream=stream),
        grid=(stream.tiles,),
        in_specs=[_tok_spec(D_MODEL), *_halo_specs(D_MODEL, F_HALO, stream), _mod_spec(layer, stream),
                  _full((1, D_MODEL)), _full((1, D_MODEL)), _full((D_MODEL, 2 * FFN_HIDDEN)),
                  _full((SUBLANES, FFN_HIDDEN)), _full((1, FFN_HIDDEN)), _full((FFN_HIDDEN, D_MODEL))],
        out_specs=_tok_spec(D_MODEL),
        out_shape=jax.ShapeDtypeStruct((stream.rows, D_MODEL), F32),
        scratch_shapes=[pltpu.VMEM((TM + 2 * F_HALO, FFN_CHUNK), F32), pltpu.VMEM((TM, FFN_HIDDEN), BF16)],
        compiler_params=_params(1),
        name="conv_ffn",
    )(xs, xs, xs, mods, g_pre, g_post, w_up, conv_w, conv_b, w_down)


def _pad_rows(w, rows):
    return jnp.concatenate([w, jnp.zeros((rows - w.shape[0],) + w.shape[1:], w.dtype)], axis=0)


def _pad_last(w, width):
    return jnp.concatenate([w, jnp.zeros(w.shape[:-1] + (width - w.shape[-1],), w.dtype)], axis=-1)


def _mla_weights(w_in, w_uq, w_ukv, w_out):
    rank = C_Q_RANK + C_KV_RANK
    pe = jnp.concatenate([jnp.zeros((D_MODEL, C_NOPE), F32), w_in[:, rank:],
                          jnp.zeros((D_MODEL, HEAD_PAD - C_NOPE - C_ROPE), F32)], axis=1)
    w_in_p = jnp.concatenate([w_in[:, :rank], pe], axis=1)
    w_q = _pad_last(w_uq.reshape(C_Q_RANK, C_HEADS, C_NOPE + C_ROPE), HEAD_PAD).reshape(C_Q_RANK, -1)
    kv = w_ukv.reshape(C_KV_RANK, C_HEADS, C_NOPE + C_V)
    w_k = _pad_last(kv[..., :C_NOPE], HEAD_PAD).reshape(C_KV_RANK, -1)
    w_vt = kv[..., C_NOPE:].reshape(C_KV_RANK, -1).T
    return tuple(w.astype(BF16) for w in (w_in_p, w_q, w_k, w_vt, w_out))


def kernel(x, c, ctx, c_ctx, ada_w, ada_b, norm_mix_pre, norm_mix_post, norm_ffn_pre, norm_ffn_post, ffn_w_up,
           ffn_conv_w, ffn_conv_b, ffn_w_down, ab_w_in, a_conv_w, a_conv_b, a_ln_g, a_ln_b, b_lambda, b_subln,
           ab_w_out, c_w_in, c_q_norm, c_kv_norm, c_w_uq, c_w_ukv, c_w_out):
    mods = _modulation(c, c_ctx, ada_w, ada_b)
    stream_arrays = (ctx.reshape(BATCH * CTX_LEN, D_MODEL), x.reshape(BATCH * SEQ, D_MODEL))
    rope_b = _rope_tables(B_HEAD_DIM, 0, B_HEAD_DIM)
    rope_c = _rope_tables(C_ROPE, C_NOPE, HEAD_PAD)
    row = lambda v: v.reshape(1, -1)
    tokens = lambda t: t.reshape(BATCH, TOK, t.shape[-1])
    flat = lambda t: t.reshape(NT, t.shape[-1])

    for i in range(DEPTH):
        j = i // 2
        stream = UNIFIED if i < DEPTH - 1 else LATENT
        if i % 2 == 0:
            lambda_init = 0.8 - 0.6 * math.exp(-0.3 * i)
            a, q, k, vt = _p1_even(i, stream_arrays, mods, row(norm_mix_pre[i]), ab_w_in[j], rope_b)
            ob = _diff_attn(tokens(q), tokens(k), vt, b_lambda[j], row(b_subln[j]), lambda_init)
            xs = _post_even(i, a, flat(ob), stream_arrays, mods, ab_w_out[j].astype(BF16),
                            _pad_rows(a_conv_w[j], A_KERNEL + 1), row(a_conv_b[j]), row(a_ln_g[j]),
                            row(a_ln_b[j]), row(norm_mix_post[i]))
        else:
            w_in, w_q, w_k, w_vt, w_o = _mla_weights(c_w_in[j], c_w_uq[j], c_w_ukv[j], c_w_out[j])
            q, k, vt = _p1_odd(i, xs, mods, row(norm_mix_pre[i]), w_in, row(c_q_norm[j]), row(c_kv_norm[j]),
                               w_q, w_k, w_vt, rope_c)
            o = _mla_attn(tokens(q), tokens(k), vt)
            xs = _post_odd(i, flat(o), xs, mods, w_o, row(norm_mix_post[i]), stream)
        xs = _ffn(i, xs, mods, row(norm_ffn_pre[i]), row(norm_ffn_post[i]), ffn_w_up[i].astype(BF16),
                  _pad_rows(ffn_conv_w[i], SUBLANES), row(ffn_conv_b[i]), ffn_w_down[i].astype(BF16), stream)
        stream_arrays = (xs,)
    return xs.reshape(BATCH, SEQ, D_MODEL)
```

```python
import functools
import math
from typing import NamedTuple

import jax
import jax.numpy as jnp
from jax import lax
from jax.experimental import pallas as pl
from jax.experimental.pallas import tpu as pltpu

D_MODEL = 1024
BATCH = 8
SEQ = 4096
DEPTH = 4
CTX_LEN = 256
GRID_W = 64
ROPE_BASE = 10000.0
RMS_EPS = 1e-6
LN_EPS = 1e-5

A_WIDTH = 512
A_KERNEL = 31
B_HEADS = 4
B_HEAD_DIM = 64
B_V_DIM = 2 * B_HEAD_DIM
B_QK_COLS = B_HEADS * 2 * B_HEAD_DIM
B_WIDTH = B_HEADS * B_V_DIM
AB_IN = 2 * A_WIDTH + 2 * B_QK_COLS + B_WIDTH

C_HEADS = 16
C_NOPE = 64
C_ROPE = 32
C_V = 64
C_Q_RANK = 768
C_KV_RANK = 256

FFN_HIDDEN = 2816
FFN_KERNEL = 3

LANES = 128
SUBLANES = 8
BF16_SUBLANES = 16
MXU_COLS = 256
VMEM_LIMIT_BYTES = 56 * 1024 * 1024

TOK = CTX_LEN + SEQ
NT = BATCH * TOK
TM = 256
TPB = TOK // TM
NTILES = NT // TM
MOD_ROWS = 16
CTX_ROW = BATCH
MIX_SLOT, MIX_GATE, FFN_SLOT, FFN_GATE = 0, 2, 3, 5
HEAD_PAD = LANES
C_HEADS_PER_STEP = 4
B_HEADS_PER_STEP = 4
SCORE_CHUNK = 1024
KV_CHUNK = 256
B_SCORE_LOOKAHEAD = 3
C_SCORE_LOOKAHEAD = 2
FFN_CHUNK = MXU_COLS
A_HALO = 16
F_HALO = SUBLANES

F32 = jnp.float32
BF16 = jnp.bfloat16
LOG2_E = math.log2(math.e)


def _dot(a, b):
    return jnp.dot(a, b, preferred_element_type=F32)


def _dot_nt(a, b):
    return lax.dot_general(a, b, (((1,), (1,)), ((), ())), preferred_element_type=F32)


def _rms(x, eps=RMS_EPS):
    return x * lax.rsqrt(jnp.mean(x * x, axis=-1, keepdims=True) + eps)


def _silu(x):
    return x * jax.nn.sigmoid(x)


def _params(n_axes):
    return pltpu.CompilerParams(dimension_semantics=("parallel",) * n_axes, vmem_limit_bytes=VMEM_LIMIT_BYTES)


def _full(shape):
    return pl.BlockSpec(shape, lambda *_: (0,) * len(shape), pipeline_mode=pl.Buffered(1))


def _tile_in_batch(t):
    return t % TPB


class _Stream(NamedTuple):
    tiles_per_batch: int
    has_ctx: bool

    @property
    def tiles(self):
        return BATCH * self.tiles_per_batch

    @property
    def rows(self):
        return self.tiles * TM


UNIFIED = _Stream(TPB, True)
LATENT = _Stream(TPB - 1, False)
assert DEPTH % 2 == 0, "only the odd-layer tail has a latent-only variant"


def _mod_spec(layer, stream=UNIFIED):
    def index(t):
        row = t // stream.tiles_per_batch
        if stream.has_ctx:
            row = jnp.where(t % stream.tiles_per_batch == 0, CTX_ROW, row)
        return (layer * MOD_ROWS + row, 0, 0)
    return pl.BlockSpec((None, 6, D_MODEL), index)


def _tok_spec(width):
    return pl.BlockSpec((TM, width), lambda t: (t, 0))


def _stream_specs(separate):
    if not separate:
        return [_tok_spec(D_MODEL)]
    latent = lambda t: ((t // TPB) * LATENT.tiles_per_batch + jnp.maximum(_tile_in_batch(t) - 1, 0), 0)
    return [pl.BlockSpec((TM, D_MODEL), lambda t: (t // TPB, 0)), pl.BlockSpec((TM, D_MODEL), latent)]


def _stream_tile(refs):
    if len(refs) == 1:
        return refs[0][...]
    ctx_ref, x_ref = refs
    return jnp.where(_tile_in_batch(pl.program_id(0)) == 0, ctx_ref[...], x_ref[...])


def _latent_of_unified_spec(width):
    return pl.BlockSpec((TM, width), lambda t: ((t // LATENT.tiles_per_batch) * TPB + t % LATENT.tiles_per_batch + 1, 0))


def _halo_specs(width, rows, stream=UNIFIED):
    blocks = TM // rows
    prev = lambda t: (jnp.maximum(t * blocks - 1, 0), 0)
    nxt = lambda t: (jnp.minimum((t + 1) * blocks, stream.rows // rows - 1), 0)
    return [pl.BlockSpec((rows, width), prev), pl.BlockSpec((rows, width), nxt)]


def _halo_valid(stream):
    r = pl.program_id(0) % stream.tiles_per_batch
    first_latent = 1 if stream.has_ctx else 0
    return ((r > first_latent).astype(F32),
            ((r >= first_latent) & (r < stream.tiles_per_batch - 1)).astype(F32))


def _vt_spec(heads, rows):
    return pl.BlockSpec((None, heads, rows, TM), lambda t: (t // TPB, 0, 0, _tile_in_batch(t)))


def _store_vt(vt_ref, vt, heads, dim):
    for h in range(heads):
        vt_ref[h, 0:dim, :] = vt[h * dim:(h + 1) * dim, :].astype(BF16)
        vt_ref[h, dim:dim + BF16_SUBLANES, :] = jnp.ones((BF16_SUBLANES, TM), BF16)


def _rope_spec():
    return pl.BlockSpec((TM, LANES), lambda t: (_tile_in_batch(t), 0))


def _rope(y, c, s_up, s_dn, shift):
    return y * c + pltpu.roll(y, LANES - shift, 1) * s_up + pltpu.roll(y, shift, 1) * s_dn


def _rope_tables(group, lane_offset, period):
    half = group // 4
    lane = jnp.arange(LANES)
    rel = (lane % period) - lane_offset
    active = (rel >= 0) & (rel < group)
    rel = jnp.where(active, rel, 0)
    by_col = rel >= group // 2
    within = rel % (group // 2)
    freq_idx = within % half
    inv_freq = ROPE_BASE ** (-freq_idx.astype(F32) / half)
    t = jnp.arange(SEQ, dtype=jnp.int32)
    pos = jnp.where(by_col[None, :], (t % GRID_W)[:, None], (t // GRID_W)[:, None]).astype(F32)
    ang = pos * inv_freq[None, :]
    cos, sin = jnp.cos(ang), jnp.sin(ang)
    first = within < half
    c = jnp.where(active[None, :], cos, 1.0)
    s_up = jnp.where((active & first)[None, :], -sin, 0.0)
    s_dn = jnp.where((active & ~first)[None, :], sin, 0.0)
    ident = [jnp.ones((CTX_LEN, LANES), F32), jnp.zeros((CTX_LEN, LANES), F32), jnp.zeros((CTX_LEN, LANES), F32)]
    return tuple(jnp.concatenate([i, x.astype(F32)], axis=0) for i, x in zip(ident, (c, s_up, s_dn)))


def _mods_kernel(c_ref, w_ref, b_ref, o_ref):
    s = _silu(c_ref[...])
    o_ref[...] = jnp.dot(s, w_ref[...], preferred_element_type=F32, precision=lax.Precision.HIGHEST) + b_ref[...]


def _modulation(c, c_ctx, ada_w, ada_b):
    rows = jnp.concatenate([c, c_ctx[None, :], jnp.zeros((MOD_ROWS - BATCH - 1, D_MODEL), F32)], axis=0)
    n_blk = 1536
    out = pl.pallas_call(
        _mods_kernel,
        grid=(DEPTH, 6 * D_MODEL // n_blk),
        in_specs=[
            pl.BlockSpec((MOD_ROWS, D_MODEL), lambda i, n: (0, 0)),
            pl.BlockSpec((None, D_MODEL, n_blk), lambda i, n: (i, 0, n)),
            pl.BlockSpec((None, 1, n_blk), lambda i, n: (i, 0, n)),
        ],
        out_specs=pl.BlockSpec((None, MOD_ROWS, n_blk), lambda i, n: (i, 0, n)),
        out_shape=jax.ShapeDtypeStruct((DEPTH, MOD_ROWS, 6 * D_MODEL), F32),
        compiler_params=_params(2),
        name="modulation",
    )(rows, ada_w, ada_b.reshape(DEPTH, 1, 6 * D_MODEL))
    return out.reshape(DEPTH * MOD_ROWS, 6, D_MODEL)


def _pre_norm(x, g_ref, mod_ref, slot):
    return _rms(x) * (g_ref[...] * (1.0 + mod_ref[slot + 1:slot + 2, :])) + mod_ref[slot:slot + 1, :]


def _p1_even_kernel(*refs):
    mod_ref, g_ref, w_ref, wvt_ref, rc_ref, ru_ref, rd_ref, a_ref, q_ref, k_ref, vt_ref = refs[-11:]
    hb = _pre_norm(_stream_tile(refs[:-11]), g_ref, mod_ref, MIX_SLOT).astype(BF16)
    a = _dot(hb, w_ref[:, 0:2 * A_WIDTH])
    a_ref[...] = (a[:, :A_WIDTH] * jax.nn.sigmoid(a[:, A_WIDTH:])).astype(BF16)
    c, s_up, s_dn = rc_ref[...], ru_ref[...], rd_ref[...]
    o = 2 * A_WIDTH
    for col, out_ref, scale in ((o, q_ref, B_HEAD_DIM ** -0.5 * LOG2_E), (o + B_QK_COLS, k_ref, 1.0)):
        y = _dot(hb, w_ref[:, col:col + B_QK_COLS])
        for j in range(B_QK_COLS // LANES):
            yj = _rope(y[:, j * LANES:(j + 1) * LANES], c, s_up, s_dn, B_HEAD_DIM // 4)
            out_ref[:, j * LANES:(j + 1) * LANES] = (yj * scale).astype(BF16)
    _store_vt(vt_ref, _dot_nt(wvt_ref[...], hb), B_HEADS, B_V_DIM)


def _p1_even(layer, stream_arrays, mods, g, w_in, rope):
    qk_cols = 2 * A_WIDTH + 2 * B_QK_COLS
    vt_rows = B_V_DIM + BF16_SUBLANES
    return pl.pallas_call(
        _p1_even_kernel,
        grid=(NTILES,),
        in_specs=[*_stream_specs(len(stream_arrays) == 2), _mod_spec(layer), _full((1, D_MODEL)),
                  _full((D_MODEL, qk_cols)), _full((B_WIDTH, D_MODEL)), _rope_spec(), _rope_spec(), _rope_spec()],
        out_specs=[_tok_spec(A_WIDTH), _tok_spec(B_QK_COLS), _tok_spec(B_QK_COLS), _vt_spec(B_HEADS, vt_rows)],
        out_shape=[jax.ShapeDtypeStruct((NT, A_WIDTH), BF16), jax.ShapeDtypeStruct((NT, B_QK_COLS), BF16),
                   jax.ShapeDtypeStruct((NT, B_QK_COLS), BF16),
                   jax.ShapeDtypeStruct((BATCH, B_HEADS, vt_rows, TOK), BF16)],
        compiler_params=_params(1),
        name="p1_even",
    )(*stream_arrays, mods, g, w_in[:, :qk_cols].astype(BF16), w_in[:, qk_cols:].T.astype(BF16), *rope)


def _softmax_step(st, vt, state):
    m_chunk = jnp.max(st, axis=0, keepdims=True)
    m_new = m_chunk if state is None else jnp.maximum(state[0], m_chunk)
    pv = _dot(vt, jnp.exp2((st - m_new).astype(BF16)))
    if state is None:
        return m_new, pv
    return m_new, jnp.exp2(state[0] - m_new) * state[1] + pv


def _attend(streams, lookahead, join_context, scores, values, finish):
    ctx_only = [(0, CTX_LEN)]
    latent = [(CTX_LEN + i * SCORE_CHUNK, SCORE_CHUNK) for i in range(SEQ // SCORE_CHUNK)]
    everything = [(0, CTX_LEN + SCORE_CHUNK)] + latent[1:] if join_context else ctx_only + latent
    qi = pl.program_id(2)
    for chunks, cond in ((ctx_only, qi == 0), (everything, qi > 0)):
        items = [(start, size, s) for start, size in chunks for s in range(streams)]

        @pl.when(cond)
        def _(items=items):
            state = [None] * streams
            pending = [scores(*item) for item in items[:lookahead]]
            for n, (start, size, s) in enumerate(items):
                if n + lookahead < len(items):
                    pending.append(scores(*items[n + lookahead]))
                st = pending.pop(0)
                step = min(KV_CHUNK, size)
                for j in range(0, size, step):
                    state[s] = _softmax_step(st[j:j + step], values(start + j, step, s), state[s])
            finish([acc for _, acc in state])


def _diff_attn_kernel(lam_ref, sub_ref, q_ref, k_ref, vt_ref, o_ref, *, lambda_init):
    lane = lax.broadcasted_iota(jnp.int32, (TM, LANES), 1)
    zero = jnp.zeros((TM, LANES), BF16)
    qs = []
    for h in range(B_HEADS_PER_STEP):
        q = q_ref[:, h * LANES:(h + 1) * LANES]
        qs += [jnp.where(lane < B_HEAD_DIM, q, zero), jnp.where(lane >= B_HEAD_DIM, q, zero)]

    def scores(start, size, s):
        h = s // 2
        return _dot_nt(k_ref[start:start + size, h * LANES:(h + 1) * LANES], qs[s])

    def values(start, size, s):
        return vt_ref[s // 2, :, start:start + size]

    def finish(accs):
        lv = lam_ref[...]
        lam = (jnp.exp(jnp.sum(lv[0:1] * lv[1:2], axis=-1, keepdims=True))
               - jnp.exp(jnp.sum(lv[2:3] * lv[3:4], axis=-1, keepdims=True)) + lambda_init)
        for h in range(B_HEADS_PER_STEP):
            o1, o2 = (acc[0:B_V_DIM, :] / acc[B_V_DIM:B_V_DIM + 1, :] for acc in accs[2 * h:2 * h + 2])
            o = (o1 - lam * o2).T
            o_ref[:, h * B_V_DIM:(h + 1) * B_V_DIM] = (_rms(o) * (sub_ref[...] * (1.0 - lambda_init))).astype(BF16)

    _attend(2 * B_HEADS_PER_STEP, B_SCORE_LOOKAHEAD, False, scores, values, finish)


def _diff_attn(q, k, vt, lam_vecs, subln_g, lambda_init):
    rows = B_V_DIM + BF16_SUBLANES
    width = B_HEADS_PER_STEP * LANES
    return pl.pallas_call(
        functools.partial(_diff_attn_kernel, lambda_init=lambda_init),
        grid=(BATCH, B_HEADS // B_HEADS_PER_STEP, TPB),
        in_specs=[pl.BlockSpec((4, B_HEAD_DIM), lambda b, h, i: (0, 0)),
                  pl.BlockSpec((1, B_V_DIM), lambda b, h, i: (0, 0)),
                  pl.BlockSpec((None, TM, width), lambda b, h, i: (b, i, h)),
                  pl.BlockSpec((None, TOK, width), lambda b, h, i: (b, 0, h)),
                  pl.BlockSpec((None, B_HEADS_PER_STEP, rows, TOK), lambda b, h, i: (b, h, 0, 0))],
        out_specs=pl.BlockSpec((None, TM, B_HEADS_PER_STEP * B_V_DIM), lambda b, h, i: (b, i, h)),
        out_shape=jax.ShapeDtypeStruct((BATCH, TOK, B_WIDTH), BF16),
        compiler_params=_params(3),
        name="diff_attn",
    )(lam_vecs, subln_g, q, k, vt)


def _post_even_kernel(a_ref, ap_ref, an_ref, ob_ref, *refs):
    mod_ref, w_ref, cw_ref, cb_ref, lg_ref, lb_ref, gp_ref, o_ref, u_ref = refs[-9:]
    prev_ok, next_ok = _halo_valid(UNIFIED)

    ext = TM + 2 * A_HALO
    u_ref[0, 0:A_HALO, :] = ap_ref[...].astype(F32) * prev_ok
    u_ref[0, A_HALO:A_HALO + TM, :] = a_ref[...].astype(F32)
    u_ref[0, A_HALO + TM:, :] = an_ref[...].astype(F32) * next_ok
    for s in range(1, SUBLANES):
        u_ref[s, 0:ext - SUBLANES, :] = u_ref[0, s:s + ext - SUBLANES, :]

    rows = 32
    base = A_HALO - A_KERNEL // 2
    outs = []
    for r0 in range(0, TM, rows):
        acc = jnp.broadcast_to(cb_ref[...], (rows, A_WIDTH))
        for tap in range(A_KERNEL):
            s = (base + tap) % SUBLANES
            i0 = r0 + base + tap - s
            acc = acc + cw_ref[tap:tap + 1, :] * u_ref[s, i0:i0 + rows, :]
        mu = jnp.mean(acc, axis=-1, keepdims=True)
        cen = acc - mu
        var = jnp.mean(cen * cen, axis=-1, keepdims=True)
        y = cen * lax.rsqrt(var + LN_EPS) * lg_ref[...] + lb_ref[...]
        outs.append(_silu(y).astype(BF16))
    ya = jnp.concatenate(outs, axis=0)
    y = _dot(ya, w_ref[0:A_WIDTH, :]) + _dot(ob_ref[...], w_ref[A_WIDTH:, :])
    o_ref[...] = _stream_tile(refs[:-9]) + mod_ref[MIX_GATE:MIX_GATE + 1, :] * (_rms(y) * gp_ref[...])


def _post_even(layer, a, ob, stream_arrays, mods, w_out, conv_w, conv_b, ln_g, ln_b, g_post):
    return pl.pallas_call(
        _post_even_kernel,
        grid=(NTILES,),
        in_specs=[_tok_spec(A_WIDTH), *_halo_specs(A_WIDTH, A_HALO),
                  _tok_spec(B_WIDTH), *_stream_specs(len(stream_arrays) == 2), _mod_spec(layer),
                  _full((A_WIDTH + B_WIDTH, D_MODEL)), _full((A_KERNEL + 1, A_WIDTH)), _full((1, A_WIDTH)),
                  _full((1, A_WIDTH)), _full((1, A_WIDTH)), _full((1, D_MODEL))],
        out_specs=_tok_spec(D_MODEL),
        out_shape=jax.ShapeDtypeStruct((NT, D_MODEL), F32),
        scratch_shapes=[pltpu.VMEM((SUBLANES, TM + 2 * A_HALO, A_WIDTH), F32)],
        compiler_params=_params(1),
        name="post_even",
    )(a, a, a, ob, *stream_arrays, mods, w_out, conv_w, conv_b, ln_g, ln_b, g_post)


def _p1_odd_kernel(x_ref, mod_ref, g_ref, w_ref, gq_ref, gkv_ref, wq_ref, wk_ref, wvt_ref,
                   rc_ref, ru_ref, rd_ref, q_ref, k_ref, vt_ref):
    hb = _pre_norm(x_ref[...], g_ref, mod_ref, MIX_SLOT).astype(BF16)
    p = _dot(hb, w_ref[...])
    c, s_up, s_dn = rc_ref[...], ru_ref[...], rd_ref[...]
    shift = C_ROPE // 4
    scale = (C_NOPE + C_ROPE) ** -0.5 * LOG2_E

    cq = (_rms(p[:, :C_Q_RANK]) * gq_ref[...]).astype(BF16)
    q = _dot(cq, wq_ref[...])
    for j in range(C_HEADS):
        sl = slice(j * HEAD_PAD, (j + 1) * HEAD_PAD)
        q_ref[:, sl] = (_rope(q[:, sl], c, s_up, s_dn, shift) * scale).astype(BF16)

    ckv = (_rms(p[:, C_Q_RANK:C_Q_RANK + C_KV_RANK]) * gkv_ref[...]).astype(BF16)
    k_pe = _rope(p[:, C_Q_RANK + C_KV_RANK:], c, s_up, s_dn, shift)
    k = _dot(ckv, wk_ref[...])
    for j in range(C_HEADS):
        sl = slice(j * HEAD_PAD, (j + 1) * HEAD_PAD)
        k_ref[:, sl] = (k[:, sl] + k_pe).astype(BF16)
    _store_vt(vt_ref, _dot_nt(wvt_ref[...], ckv), C_HEADS, C_V)


def _p1_odd(layer, xs, mods, g, w_in, g_q, g_kv, w_q, w_k, w_vt, rope):
    width = C_HEADS * HEAD_PAD
    v_width = C_HEADS * C_V
    vt_rows = C_V + BF16_SUBLANES
    in_cols = C_Q_RANK + C_KV_RANK + HEAD_PAD
    out = jax.ShapeDtypeStruct((NT, width), BF16)
    return pl.pallas_call(
        _p1_odd_kernel,
        grid=(NTILES,),
        in_specs=[_tok_spec(D_MODEL), _mod_spec(layer), _full((1, D_MODEL)),
                  _full((D_MODEL, in_cols)), _full((1, C_Q_RANK)), _full((1, C_KV_RANK)),
                  _full((C_Q_RANK, width)), _full((C_KV_RANK, width)), _full((v_width, C_KV_RANK)),
                  _rope_spec(), _rope_spec(), _rope_spec()],
        out_specs=[_tok_spec(width), _tok_spec(width), _vt_spec(C_HEADS, vt_rows)],
        out_shape=[out, out, jax.ShapeDtypeStruct((BATCH, C_HEADS, vt_rows, TOK), BF16)],
        compiler_params=_params(1),
        name="p1_odd",
    )(xs, mods, g, w_in, g_q, g_kv, w_q, w_k, w_vt, *rope)


def _mla_kernel(q_ref, k_ref, vt_ref, o_ref):
    def scores(start, size, h):
        sl = slice(h * HEAD_PAD, (h + 1) * HEAD_PAD)
        return _dot_nt(k_ref[start:start + size, sl], q_ref[:, sl])

    def values(start, size, h):
        return vt_ref[h, :, start:start + size]

    def finish(accs):
        o_t = [acc[0:C_V, :] / acc[C_V:C_V + 1, :] for acc in accs]
        o_ref[...] = jnp.concatenate(o_t, axis=0).T.astype(BF16)

    _attend(C_HEADS_PER_STEP, C_SCORE_LOOKAHEAD, True, scores, values, finish)


def _mla_attn(q, k, vt):
    width = C_HEADS_PER_STEP * HEAD_PAD
    rows = C_V + BF16_SUBLANES
    return pl.pallas_call(
        _mla_kernel,
        grid=(BATCH, C_HEADS // C_HEADS_PER_STEP, TPB),
        in_specs=[pl.BlockSpec((None, TM, width), lambda b, h, i: (b, i, h)),
                  pl.BlockSpec((None, TOK, width), lambda b, h, i: (b, 0, h)),
                  pl.BlockSpec((None, C_HEADS_PER_STEP, rows, TOK), lambda b, h, i: (b, h, 0, 0))],
        out_specs=pl.BlockSpec((None, TM, C_HEADS_PER_STEP * C_V), lambda b, h, i: (b, i, h)),
        out_shape=jax.ShapeDtypeStruct((BATCH, TOK, C_HEADS * C_V), BF16),
        compiler_params=_params(3),
        name="mla_attn",
    )(q, k, vt)


def _post_odd_kernel(o_ref, x_ref, mod_ref, w_ref, gp_ref, out_ref):
    y = _dot(o_ref[...], w_ref[...])
    out_ref[...] = x_ref[...] + mod_ref[MIX_GATE:MIX_GATE + 1, :] * (_rms(y) * gp_ref[...])


def _post_odd(layer, o, xs, mods, w_out, g_post, stream):
    width = C_HEADS * C_V
    in_spec = _tok_spec if stream.has_ctx else _latent_of_unified_spec
    return pl.pallas_call(
        _post_odd_kernel,
        grid=(stream.tiles,),
        in_specs=[in_spec(width), in_spec(D_MODEL), _mod_spec(layer, stream), _full((width, D_MODEL)),
                  _full((1, D_MODEL))],
        out_specs=_tok_spec(D_MODEL),
        out_shape=jax.ShapeDtypeStruct((stream.rows, D_MODEL), F32),
        compiler_params=_params(1),
        name="post_odd",
    )(o, xs, mods, w_out, g_post)


def _ffn_kernel(x_ref, xp_ref, xn_ref, mod_ref, g_ref, gp_ref, wu_ref, cw_ref, cb_ref, wd_ref, o_ref, g_scr, act_scr,
                *, stream):
    prev_ok, next_ok = _halo_valid(stream)
    x = x_ref[...]
    h = _pre_norm(x, g_ref, mod_ref, FFN_SLOT)
    h_prev = _pre_norm(xp_ref[...], g_ref, mod_ref, FFN_SLOT)
    h_next = _pre_norm(xn_ref[...], g_ref, mod_ref, FFN_SLOT)
    hb = h.astype(BF16)
    hb_ext = jnp.concatenate([h_prev, h, h_next], axis=0).astype(BF16)
    lo, hi = F_HALO, F_HALO + TM
    for c in range(FFN_HIDDEN // FFN_CHUNK):
        cols = slice(c * FFN_CHUNK, (c + 1) * FFN_CHUNK)
        gate = _dot(hb_ext, wu_ref[:, FFN_HIDDEN + c * FFN_CHUNK:FFN_HIDDEN + (c + 1) * FFN_CHUNK])
        g_scr[0:lo, :] = gate[0:lo] * prev_ok
        g_scr[lo:hi, :] = gate[lo:hi]
        g_scr[hi:, :] = gate[hi:] * next_ok
        conv = (cw_ref[0:1, cols] * g_scr[lo - 1:hi - 1, :] + cw_ref[1:2, cols] * gate[lo:hi]
                + cw_ref[2:3, cols] * g_scr[lo + 1:hi + 1, :] + cb_ref[:, cols])
        val = _dot(hb, wu_ref[:, cols])
        act_scr[:, cols] = (_silu(conv) * val).astype(BF16)
    y = _dot(act_scr[...], wd_ref[...])
    o_ref[...] = x + mod_ref[FFN_GATE:FFN_GATE + 1, :] * (_rms(y) * gp_ref[...])


def _ffn(layer, xs, mods, g_pre, g_post, w_up, conv_w, conv_b, w_down, stream):
    return pl.pallas_call(
        functools.partial(_ffn_kernel, stream=stream),
        grid=(stream.tiles,),
        in_specs=[_tok_spec(D_MODEL), *_halo_specs(D_MODEL, F_HALO, stream), _mod_spec(layer, stream),
                  _full((1, D_MODEL)), _full((1, D_MODEL)), _full((D_MODEL, 2 * FFN_HIDDEN)),
                  _full((SUBLANES, FFN_HIDDEN)), _full((1, FFN_HIDDEN)), _full((FFN_HIDDEN, D_MODEL))],
        out_specs=_tok_spec(D_MODEL),
        out_shape=jax.ShapeDtypeStruct((stream.rows, D_MODEL), F32),
        scratch_shapes=[pltpu.VMEM((TM + 2 * F_HALO, FFN_CHUNK), F32), pltpu.VMEM((TM, FFN_HIDDEN), BF16)],
        compiler_params=_params(1),
        name="conv_ffn",
    )(xs, xs, xs, mods, g_pre, g_post, w_up, conv_w, conv_b, w_down)


def _pad_rows(w, rows):
    return jnp.concatenate([w, jnp.zeros((rows - w.shape[0],) + w.shape[1:], w.dtype)], axis=0)


def _pad_last(w, width):
    return jnp.concatenate([w, jnp.zeros(w.shape[:-1] + (width - w.shape[-1],), w.dtype)], axis=-1)


def _mla_weights(w_in, w_uq, w_ukv, w_out):
    rank = C_Q_RANK + C_KV_RANK
    pe = jnp.concatenate([jnp.zeros((D_MODEL, C_NOPE), F32), w_in[:, rank:],
                          jnp.zeros((D_MODEL, HEAD_PAD - C_NOPE - C_ROPE), F32)], axis=1)
    w_in_p = jnp.concatenate([w_in[:, :rank], pe], axis=1)
    w_q = _pad_last(w_uq.reshape(C_Q_RANK, C_HEADS, C_NOPE + C_ROPE), HEAD_PAD).reshape(C_Q_RANK, -1)
    kv = w_ukv.reshape(C_KV_RANK, C_HEADS, C_NOPE + C_V)
    w_k = _pad_last(kv[..., :C_NOPE], HEAD_PAD).reshape(C_KV_RANK, -1)
    w_vt = kv[..., C_NOPE:].reshape(C_KV_RANK, -1).T
    return tuple(w.astype(BF16) for w in (w_in_p, w_q, w_k, w_vt, w_out))


def kernel(x, c, ctx, c_ctx, ada_w, ada_b, norm_mix_pre, norm_mix_post, norm_ffn_pre, norm_ffn_post, ffn_w_up,
           ffn_conv_w, ffn_conv_b, ffn_w_down, ab_w_in, a_conv_w, a_conv_b, a_ln_g, a_ln_b, b_lambda, b_subln,
           ab_w_out, c_w_in, c_q_norm, c_kv_norm, c_w_uq, c_w_ukv, c_w_out):
    mods = _modulation(c, c_ctx, ada_w, ada_b)
    stream_arrays = (ctx.reshape(BATCH * CTX_LEN, D_MODEL), x.reshape(BATCH * SEQ, D_MODEL))
    rope_b = _rope_tables(B_HEAD_DIM, 0, B_HEAD_DIM)
    rope_c = _rope_tables(C_ROPE, C_NOPE, HEAD_PAD)
    row = lambda v: v.reshape(1, -1)
    tokens = lambda t: t.reshape(BATCH, TOK, t.shape[-1])
    flat = lambda t: t.reshape(NT, t.shape[-1])

    for i in range(DEPTH):
        j = i // 2
        stream = UNIFIED if i < DEPTH - 1 else LATENT
        if i % 2 == 0:
            lambda_init = 0.8 - 0.6 * math.exp(-0.3 * i)
            a, q, k, vt = _p1_even(i, stream_arrays, mods, row(norm_mix_pre[i]), ab_w_in[j], rope_b)
            ob = _diff_attn(tokens(q), tokens(k), vt, b_lambda[j], row(b_subln[j]), lambda_init)
            xs = _post_even(i, a, flat(ob), stream_arrays, mods, ab_w_out[j].astype(BF16),
                            _pad_rows(a_conv_w[j], A_KERNEL + 1), row(a_conv_b[j]), row(a_ln_g[j]),
                            row(a_ln_b[j]), row(norm_mix_post[i]))
        else:
            w_in, w_q, w_k, w_vt, w_o = _mla_weights(c_w_in[j], c_w_uq[j], c_w_ukv[j], c_w_out[j])
            q, k, vt = _p1_odd(i, xs, mods, row(norm_mix_pre[i]), w_in, row(c_q_norm[j]), row(c_kv_norm[j]),
                               w_q, w_k, w_vt, rope_c)
            o = _mla_attn(tokens(q), tokens(k), vt)
            xs = _post_odd(i, flat(o), xs, mods, w_o, row(norm_mix_post[i]), stream)
        xs = _ffn(i, xs, mods, row(norm_ffn_pre[i]), row(norm_ffn_post[i]), ffn_w_up[i].astype(BF16),
                  _pad_rows(ffn_conv_w[i], SUBLANES), row(ffn_conv_b[i]), ffn_w_down[i].astype(BF16), stream)
        stream_arrays = (xs,)
    return xs.reshape(BATCH, SEQ, D_MODEL)
```
